```python
import math
import jax, jax.numpy as jnp
from jax import lax
import numpy as np

D_MODEL = 2048
BATCH = 4
SEQ = 4096
DEPTH = 2

CHUNK = 64
D_MIX = D_MODEL
HEAD_DIM = 128
LRU_HEADS = 6
D_LRU = LRU_HEADS * HEAD_DIM
CONV_A = 4
LRU_C = 8.0
CONV_GROUPS = 4
D_CONV = CONV_GROUPS * HEAD_DIM
CONV_B = 3
SB_HEADS = 6
D_SB = SB_HEADS * HEAD_DIM
Q_BLOCK = 128
IN_SPLIT_SIZES = (D_LRU, D_LRU, D_CONV, D_CONV, D_CONV, D_SB, D_SB, D_SB)
D_IN = 2 * D_LRU + 3 * D_CONV + 3 * D_SB
N_GROUPS = 4
EXPERTS_PER_GROUP = 8
N_EXPERTS = N_GROUPS * EXPERTS_PER_GROUP
TOP_K = 2
D_EXPERT = 512
ROW_BLOCK = 128
EPS = 1e-6

kernel_name = 'hymba_style_rglru_shortconv_stickbreak_hiermoe'


def rmsnorm(x, g):
    xf = x.astype(jnp.float32)
    y = xf * lax.rsqrt(jnp.mean(xf * xf, axis=-1, keepdims=True) + EPS)
    return (y * g.astype(jnp.float32)).astype(x.dtype)


def causal_depthwise_conv(x, w):
    k = w.shape[0]
    return lax.conv_general_dilated(x, w[:, None, :].astype(x.dtype), window_strides=(1,),
                                    padding=[(k - 1, 0)],
                                    dimension_numbers=('NWC', 'WIO', 'NWC'),
                                    feature_group_count=x.shape[-1])


def _linear_combine(c1, c2):
    a1, b1 = c1
    a2, b2 = c2
    return a1 * a2, a2 * b1 + b2


def rg_lru(x, w_a, b_a, w_x, b_x, lam):
    bsz, s, _ = x.shape
    xf = x.astype(jnp.float32)
    xh = xf.reshape(bsz, s, LRU_HEADS, HEAD_DIM)
    r = jax.nn.sigmoid(jnp.einsum('bshi,hij->bshj', xh, w_a.astype(jnp.float32)).reshape(bsz, s, D_LRU)
                       + b_a.astype(jnp.float32))
    i = jax.nn.sigmoid(jnp.einsum('bshi,hij->bshj', xh, w_x.astype(jnp.float32)).reshape(bsz, s, D_LRU)
                       + b_x.astype(jnp.float32))
    log_a = -LRU_C * r * jax.nn.softplus(-lam.astype(jnp.float32))
    a = jnp.exp(log_a)
    u = jnp.sqrt(-jnp.expm1(2.0 * log_a)) * (i * xf)
    _, h = lax.associative_scan(_linear_combine, (a, u), axis=1)
    return h.astype(x.dtype)


def stick_breaking_attention(q, k, v):
    s_len = q.shape[2]
    scale = 1.0 / math.sqrt(HEAD_DIM)
    outs = []
    for blk in range(s_len // Q_BLOCK):
        start = blk * Q_BLOCK
        end = start + Q_BLOCK
        qb = q[:, :, start:end]
        kb = k[:, :, :end]
        vb = v[:, :, :end]
        z = jnp.einsum('bhqd,bhkd->bhqk', qb, kb) * scale
        t_pos = start + jnp.arange(Q_BLOCK)[:, None]
        s_pos = jnp.arange(end)[None, :]
        mask = s_pos < t_pos
        log_keep = jnp.where(mask, jax.nn.log_sigmoid(-z), 0.0)
        suffix = lax.cumsum(log_keep, axis=3, reverse=True) - log_keep
        w = jnp.where(mask, jnp.exp(jax.nn.log_sigmoid(z) + suffix), 0.0)
        outs.append(jnp.einsum('bhqk,bhkd->bhqd', w, vb))
    return jnp.concatenate(outs, axis=2)


def hierarchical_moe(h, w_rg, b_rg, w_re, b_re, w_gate, w_up, w_down):
    bsz, s, d = h.shape
    t = bsz * s
    xt = h.reshape(t, d)
    grp_logits = (xt @ w_rg).astype(jnp.float32) + b_rg.astype(jnp.float32)
    grp_prob = jax.nn.softmax(grp_logits, axis=-1)
    g_idx = jnp.argmax(grp_logits, axis=-1)
    p_g = jnp.take_along_axis(grp_prob, g_idx[:, None], axis=-1)[:, 0]
    exp_logits = ((xt @ w_re).astype(jnp.float32) + b_re.astype(jnp.float32)).reshape(t, N_GROUPS, EXPERTS_PER_GROUP)
    in_grp = jnp.take_along_axis(exp_logits, g_idx[:, None, None], axis=1)[:, 0]
    top_p, top_i = lax.top_k(jax.nn.softmax(in_grp, axis=-1), TOP_K)
    top_p = top_p / jnp.sum(top_p, axis=-1, keepdims=True)
    gates = p_g[:, None] * top_p
    expert_ids = g_idx[:, None] * EXPERTS_PER_GROUP + top_i

    tk = t * TOP_K
    flat_e = expert_ids.reshape(tk).astype(jnp.int32)
    flat_g = gates.reshape(tk)
    flat_tok = jnp.repeat(jnp.arange(t, dtype=jnp.int32), TOP_K)
    order = jnp.argsort(flat_e)
    e_sorted = flat_e[order]
    tok_sorted = flat_tok[order]
    g_sorted = flat_g[order]
    counts = jnp.zeros((N_EXPERTS,), jnp.int32).at[flat_e].add(1)
    padded = (counts + ROW_BLOCK - 1) // ROW_BLOCK * ROW_BLOCK
    starts = jnp.cumsum(counts) - counts
    pends = jnp.cumsum(padded)
    pstarts = pends - padded
    dest = pstarts[e_sorted] + (jnp.arange(tk, dtype=jnp.int32) - starts[e_sorted])
    n_blocks = -(-tk // ROW_BLOCK) + N_EXPERTS
    rows = n_blocks * ROW_BLOCK
    x_pad = jnp.zeros((rows, d), xt.dtype).at[dest].set(xt[tok_sorted])
    tok_pad = jnp.full((rows,), t, jnp.int32).at[dest].set(tok_sorted)
    gate_pad = jnp.zeros((rows,), jnp.float32).at[dest].set(g_sorted)
    block_start = jnp.arange(n_blocks, dtype=jnp.int32) * ROW_BLOCK
    block_expert = jnp.minimum(jnp.searchsorted(pends, block_start, side='right'), N_EXPERTS - 1)

    def expert_block(args):
        xb, e = args
        hid = jax.nn.silu(xb @ w_gate[e]) * (xb @ w_up[e])
        return hid @ w_down[e]

    y_pad = lax.map(expert_block, (x_pad.reshape(n_blocks, ROW_BLOCK, d), block_expert))
    y_pad = y_pad.reshape(rows, d) * gate_pad[:, None].astype(y_pad.dtype)
    out = jax.ops.segment_sum(y_pad, tok_pad, num_segments=t + 1)[:t]
    return out.reshape(bsz, s, d)


def setup_inputs(seed: int = 0) -> dict:
    key = jax.random.key(seed)
    ks = jax.random.split(key, 24)
    f32 = jnp.float32
    nrm = lambda k, shape, sc: jax.random.normal(k, shape, f32) * sc
    u = jax.random.uniform(ks[9], (DEPTH, D_LRU), f32, minval=0.9, maxval=0.999)
    sig = u ** (1.0 / LRU_C)
    lam = jnp.log(sig) - jnp.log1p(-sig)
    return {
        'x': jax.random.normal(ks[0], (BATCH, SEQ, D_MODEL), f32),
        'norm_mix': 1.0 + nrm(ks[1], (DEPTH, D_MODEL), 0.1),
        'w_in': nrm(ks[2], (DEPTH, D_MODEL, D_IN), D_MODEL ** -0.5),
        'conv_a_w': nrm(ks[3], (DEPTH, CONV_A, D_LRU), CONV_A ** -0.5),
        'conv_a_b': nrm(ks[4], (DEPTH, D_LRU), 0.02),
        'lru_wa': nrm(ks[5], (DEPTH, LRU_HEADS, HEAD_DIM, HEAD_DIM), HEAD_DIM ** -0.5),
        'lru_ba': nrm(ks[6], (DEPTH, D_LRU), 0.02),
        'lru_wx': nrm(ks[7], (DEPTH, LRU_HEADS, HEAD_DIM, HEAD_DIM), HEAD_DIM ** -0.5),
        'lru_bx': nrm(ks[8], (DEPTH, D_LRU), 0.02),
        'lru_lam': lam,
        'conv_b_w': nrm(ks[10], (DEPTH, CONV_B, D_CONV), CONV_B ** -0.5),
        'q_norm': 1.0 + nrm(ks[11], (DEPTH, HEAD_DIM), 0.1),
        'k_norm': 1.0 + nrm(ks[12], (DEPTH, HEAD_DIM), 0.1),
        'mix_norm': 1.0 + nrm(ks[13], (DEPTH, D_MIX), 0.1),
        'w_out': nrm(ks[14], (DEPTH, D_MIX, D_MODEL), D_MIX ** -0.5),
        'norm_ffn': 1.0 + nrm(ks[15], (DEPTH, D_MODEL), 0.1),
        'w_router_group': nrm(ks[16], (DEPTH, D_MODEL, N_GROUPS), D_MODEL ** -0.5),
        'b_router_group': nrm(ks[17], (DEPTH, N_GROUPS), 0.01),
        'w_router_expert': nrm(ks[18], (DEPTH, D_MODEL, N_EXPERTS), D_MODEL ** -0.5),
        'b_router_expert': nrm(ks[19], (DEPTH, N_EXPERTS), 0.01),
        'w_gate': nrm(ks[20], (DEPTH, N_EXPERTS, D_MODEL, D_EXPERT), D_MODEL ** -0.5),
        'w_up': nrm(ks[21], (DEPTH, N_EXPERTS, D_MODEL, D_EXPERT), D_MODEL ** -0.5),
        'w_down': nrm(ks[22], (DEPTH, N_EXPERTS, D_EXPERT, D_MODEL), D_EXPERT ** -0.5),
    }


def reference(x, norm_mix, w_in, conv_a_w, conv_a_b, lru_wa, lru_ba, lru_wx, lru_bx, lru_lam,
              conv_b_w, q_norm, k_norm, mix_norm, w_out, norm_ffn, w_router_group, b_router_group,
              w_router_expert, b_router_expert, w_gate, w_up, w_down):
    bsz, s, _ = x.shape
    split_idx = [int(c) for c in np.cumsum(IN_SPLIT_SIZES)[:-1]]
    for l in range(DEPTH):
        h = rmsnorm(x, norm_mix[l])
        proj = h @ w_in[l]
        xa, ga, bg, cg, xb, q, k, v = jnp.split(proj, split_idx, axis=-1)
        xa = causal_depthwise_conv(xa, conv_a_w[l]) + conv_a_b[l].astype(xa.dtype)
        ya = rg_lru(xa, lru_wa[l], lru_ba[l], lru_wx[l], lru_bx[l], lru_lam[l]) * jax.nn.gelu(ga)
        yb = bg * causal_depthwise_conv(cg * xb, conv_b_w[l])
        to_heads = lambda z: z.reshape(bsz, s, SB_HEADS, HEAD_DIM).transpose(0, 2, 1, 3).astype(jnp.float32)
        qh = to_heads(rmsnorm(q.reshape(bsz, s, SB_HEADS, HEAD_DIM), q_norm[l]))
        kh = to_heads(rmsnorm(k.reshape(bsz, s, SB_HEADS, HEAD_DIM), k_norm[l]))
        vh = to_heads(v)
        yc = stick_breaking_attention(qh, kh, vh).transpose(0, 2, 1, 3).reshape(bsz, s, D_SB).astype(x.dtype)
        y = jnp.concatenate([ya, yb, yc], axis=-1)
        y = rmsnorm(y.reshape(bsz, s, D_MIX // HEAD_DIM, HEAD_DIM), jnp.ones((HEAD_DIM,), jnp.float32))
        y = y.reshape(bsz, s, D_MIX) * mix_norm[l].astype(y.dtype)
        x = x + y @ w_out[l]
        h2 = rmsnorm(x, norm_ffn[l])
        x = x + hierarchical_moe(h2, w_router_group[l], b_router_group[l], w_router_expert[l],
                                 b_router_expert[l], w_gate[l], w_up[l], w_down[l])
    return x
```

```python
import functools
import math

import jax
import jax.numpy as jnp
from jax import lax
from jax.experimental import pallas as pl
from jax.experimental.pallas import tpu as pltpu

EPS = 1e-6
LRU_C = 8.0
HEAD_DIM = 128
SUBLANES = 8
EXPERTS_PER_GROUP = 8
ROUTE_ROWS = 48
VMEM_LIMIT = 48 * 1024 * 1024

F32 = jnp.float32
BF16 = jnp.bfloat16


def _tile(n, pref):
    if n <= pref:
        return n
    t = pref
    while n % t:
        t -= SUBLANES
    assert t > 0
    return t


def _params(sem):
    return pltpu.CompilerParams(dimension_semantics=sem, vmem_limit_bytes=VMEM_LIMIT)


def _rms(v):
    return v * lax.rsqrt(jnp.mean(v * v, axis=-1, keepdims=True) + EPS)


def _inproj_kernel(x_ref, g_ref, w_ref, qn_ref, kn_ref, mix_ref, qkv_ref, h_scr, *, n_mix, heads):
    j = pl.program_id(1)

    @pl.when(j == 0)
    def _():
        h_scr[...] = (_rms(x_ref[...]) * g_ref[...]).astype(BF16)

    acc = jnp.dot(h_scr[...], w_ref[...], preferred_element_type=F32)

    @pl.when(j < n_mix)
    def _():
        mix_ref[...] = acc

    def head_norm(gain_ref):
        for h in range(heads):
            sl = slice(h * HEAD_DIM, (h + 1) * HEAD_DIM)
            qkv_ref[:, sl] = (_rms(acc[:, sl]) * gain_ref[...]).astype(BF16)

    @pl.when(j == n_mix)
    def _():
        head_norm(qn_ref)

    @pl.when(j == n_mix + 1)
    def _():
        head_norm(kn_ref)

    @pl.when(j == n_mix + 2)
    def _():
        qkv_ref[...] = acc.astype(BF16)


def _in_projection(x, gain, w_bf16, q_gain, k_gain, d_sb):
    t, d = x.shape
    d_in = w_bf16.shape[1]
    tn = d_sb
    assert d_in % tn == 0 and (d_in - 3 * d_sb) % tn == 0
    n_mix = (d_in - 3 * d_sb) // tn
    tm = _tile(t, 512)
    kern = functools.partial(_inproj_kernel, n_mix=n_mix, heads=d_sb // HEAD_DIM)
    return pl.pallas_call(
        kern,
        grid=(t // tm, d_in // tn),
        in_specs=[
            pl.BlockSpec((tm, d), lambda i, j: (i, 0)),
            pl.BlockSpec((1, d), lambda i, j: (0, 0)),
            pl.BlockSpec((d, tn), lambda i, j: (0, j)),
            pl.BlockSpec((1, HEAD_DIM), lambda i, j: (0, 0)),
            pl.BlockSpec((1, HEAD_DIM), lambda i, j: (0, 0)),
        ],
        out_specs=[
            pl.BlockSpec((tm, tn), lambda i, j: (i, jnp.minimum(j, n_mix - 1))),
            pl.BlockSpec((tm, tn), lambda i, j: (i, jnp.maximum(j - n_mix, 0))),
        ],
        out_shape=[
            jax.ShapeDtypeStruct((t, n_mix * tn), F32),
            jax.ShapeDtypeStruct((t, 3 * d_sb), BF16),
        ],
        scratch_shapes=[pltpu.VMEM((tm, d), BF16)],
        compiler_params=_params(("arbitrary", "arbitrary")),
        name="in_projection",
    )(x, gain, w_bf16, q_gain, k_gain)


def _softplus(v):
    return jnp.maximum(v, 0.0) + jnp.log1p(jnp.exp(-jnp.abs(v)))


def _mixer_a_kernel(xa_ref, ga_ref, cw_ref, cb_ref, wg_ref, ba_ref, bx_ref, lam_ref, mn_ref,
                    out_ref, ext_scr, a_scr, u_scr, hc_scr, *, ts, taps):
    si = pl.program_id(2)

    @pl.when(si == 0)
    def _():
        ext_scr[0:SUBLANES, :] = jnp.zeros((SUBLANES, HEAD_DIM), F32)
        hc_scr[...] = jnp.zeros((SUBLANES, HEAD_DIM), F32)

    ext_scr[SUBLANES:SUBLANES + ts, :] = xa_ref[0]
    cw = cw_ref[...]
    xc = cb_ref[...]
    for k in range(taps):
        lo = SUBLANES - (taps - 1) + k
        xc = xc + cw[k:k + 1, :] * ext_scr[lo:lo + ts, :]
    ext_scr[0:SUBLANES, :] = ext_scr[ts:ts + SUBLANES, :]

    gates = jnp.dot(xc.astype(BF16), wg_ref[0], preferred_element_type=F32)
    r = jax.nn.sigmoid(gates[:, :HEAD_DIM] + ba_ref[...])
    i = jax.nn.sigmoid(gates[:, HEAD_DIM:] + bx_ref[...])
    log_a = (-LRU_C) * r * _softplus(-lam_ref[...])
    a = jnp.exp(log_a)
    u = jnp.sqrt(-jnp.tanh(log_a) * (a * a + 1.0)) * (i * xc)

    sub = lax.broadcasted_iota(jnp.int32, (ts, HEAD_DIM), 0) & (SUBLANES - 1)
    d = 1
    while d < SUBLANES:
        keep = sub >= d
        a_sh = jnp.where(keep, pltpu.roll(a, d, 0), 1.0)
        u_sh = jnp.where(keep, pltpu.roll(u, d, 0), 0.0)
        u = a * u_sh + u
        a = a * a_sh
        d *= 2
    a_scr[...] = a
    u_scr[...] = u

    def carry_step(c, hb):
        off = pl.multiple_of(c * SUBLANES, SUBLANES)
        h = u_scr[pl.ds(off, SUBLANES), :] + a_scr[pl.ds(off, SUBLANES), :] * hb
        u_scr[pl.ds(off, SUBLANES), :] = h
        return jnp.broadcast_to(h[SUBLANES - 1:SUBLANES, :], (SUBLANES, HEAD_DIM))

    hc_scr[...] = lax.fori_loop(0, ts // SUBLANES, carry_step, hc_scr[...], unroll=8)

    y = u_scr[...] * jax.nn.gelu(ga_ref[0])
    out_ref[0] = (_rms(y) * mn_ref[...]).astype(BF16)


def _mixer_a(mix3, conv_w, conv_b, w_gates, b_a, b_x, lam, mix_gain, heads):
    b, s, _ = mix3.shape
    ts = _tile(s, 512)
    taps = conv_w.shape[0]
    assert taps - 1 <= SUBLANES and ts % SUBLANES == 0
    kern = functools.partial(_mixer_a_kernel, ts=ts, taps=taps)
    vec = lambda: pl.BlockSpec((1, HEAD_DIM), lambda bi, h, si: (0, h))
    return pl.pallas_call(
        kern,
        grid=(b, heads, s // ts),
        in_specs=[
            pl.BlockSpec((1, ts, HEAD_DIM), lambda bi, h, si: (bi, si, h)),
            pl.BlockSpec((1, ts, HEAD_DIM), lambda bi, h, si: (bi, si, heads + h)),
            pl.BlockSpec((taps, HEAD_DIM), lambda bi, h, si: (0, h)),
            vec(),
            pl.BlockSpec((1, HEAD_DIM, 2 * HEAD_DIM), lambda bi, h, si: (h, 0, 0)),
            vec(), vec(), vec(), vec(),
        ],
        out_specs=pl.BlockSpec((1, ts, HEAD_DIM), lambda bi, h, si: (bi, si, h)),
        out_shape=jax.ShapeDtypeStruct((b, s, heads * HEAD_DIM), BF16),
        scratch_shapes=[
            pltpu.VMEM((ts + SUBLANES, HEAD_DIM), F32),
            pltpu.VMEM((ts, HEAD_DIM), F32),
            pltpu.VMEM((ts, HEAD_DIM), F32),
            pltpu.VMEM((SUBLANES, HEAD_DIM), F32),
        ],
        compiler_params=_params(("arbitrary", "arbitrary", "arbitrary")),
        name="mixer_rglru",
    )(mix3, mix3, conv_w, conv_b, w_gates, b_a, b_x, lam, mix_gain)


def _mixer_b_kernel(bg_ref, cg_ref, xb_ref, cw_ref, mn_ref, out_ref, ext_scr, *, ts, taps):
    si = pl.program_id(2)

    @pl.when(si == 0)
    def _():
        ext_scr[0:SUBLANES, :] = jnp.zeros((SUBLANES, HEAD_DIM), F32)

    ext_scr[SUBLANES:SUBLANES + ts, :] = cg_ref[0] * xb_ref[0]
    cw = cw_ref[...]
    conv = jnp.zeros((ts, HEAD_DIM), F32)
    for k in range(taps):
        lo = SUBLANES - (taps - 1) + k
        conv = conv + cw[k:k + 1, :] * ext_scr[lo:lo + ts, :]
    ext_scr[0:SUBLANES, :] = ext_scr[ts:ts + SUBLANES, :]
    y = bg_ref[0] * conv
    out_ref[0] = (_rms(y) * mn_ref[...]).astype(BF16)


def _mixer_b(mix3, conv_w, mix_gain, col0, groups):
    b, s, _ = mix3.shape
    ts = _tile(s, 512)
    taps = conv_w.shape[0]
    assert taps - 1 <= SUBLANES and ts % SUBLANES == 0
    kern = functools.partial(_mixer_b_kernel, ts=ts, taps=taps)
    blk = lambda off: pl.BlockSpec((1, ts, HEAD_DIM), lambda bi, g, si: (bi, si, off + g))
    return pl.pallas_call(
        kern,
        grid=(b, groups, s // ts),
        in_specs=[
            blk(col0), blk(col0 + groups), blk(col0 + 2 * groups),
            pl.BlockSpec((taps, HEAD_DIM), lambda bi, g, si: (0, g)),
            pl.BlockSpec((1, HEAD_DIM), lambda bi, g, si: (0, g)),
        ],
        out_specs=pl.BlockSpec((1, ts, HEAD_DIM), lambda bi, g, si: (bi, si, g)),
        out_shape=jax.ShapeDtypeStruct((b, s, groups * HEAD_DIM), BF16),
        scratch_shapes=[pltpu.VMEM((ts + SUBLANES, HEAD_DIM), F32)],
        compiler_params=_params(("arbitrary", "arbitrary", "arbitrary")),
        name="mixer_gated_conv",
    )(mix3, mix3, mix3, conv_w, mix_gain)


def _attn_kernel(q_ref, k_ref, v_ref, m_ref, mn_ref, o_ref, *, bq, bk, scale):
    qi = pl.program_id(2)
    q = q_ref[0]
    n_kb = (qi + 1) * (bq // bk)
    row = qi * bq + lax.broadcasted_iota(jnp.int32, (bq, bk), 0)
    lane = lax.broadcasted_iota(jnp.int32, (bq, bk), 1)

    def kv_step(j, carry):
        acc, run = carry
        off = pl.multiple_of((n_kb - 1 - j) * bk, bk)
        kb = k_ref[0, pl.ds(off, bk), :]
        vb = v_ref[0, pl.ds(off, bk), :]
        z = lax.dot_general(q, kb, (((1,), (1,)), ((), ())), preferred_element_type=F32) * scale
        ls = jnp.minimum(z, 0.0) - jnp.log1p(jnp.exp(-jnp.abs(z)))
        mask = (off + lane) < row
        lk = jnp.where(mask, ls - z, 0.0)
        hi = lk.astype(BF16)
        lo = (lk - hi.astype(F32)).astype(BF16)
        sums = jnp.dot(jnp.concatenate([hi, lo], axis=1), m_ref[...], preferred_element_type=F32)
        w = jnp.where(mask, jnp.exp(ls + sums[:, :bk] + run), 0.0)
        acc = acc + jnp.dot(w.astype(BF16), vb, preferred_element_type=F32)
        return acc, run + sums[:, bk:]

    zero = jnp.zeros((bq, HEAD_DIM), F32)
    acc, _ = lax.fori_loop(0, n_kb, kv_step, (zero, zero))
    o_ref[0] = (_rms(acc) * mn_ref[...]).astype(BF16)


def _suffix_matrix(bk):
    j = jnp.arange(bk)[:, None]
    s = jnp.arange(bk)[None, :]
    half = jnp.concatenate([(j > s).astype(BF16), jnp.ones((bk, HEAD_DIM), BF16)], axis=1)
    return jnp.concatenate([half, half], axis=0)


def _attention(qkv3, mix_gain, heads):
    b, s, _ = qkv3.shape
    bk = HEAD_DIM
    bq = _tile(s, 256)
    assert bq % bk == 0 and s % bq == 0
    kern = functools.partial(_attn_kernel, bq=bq, bk=bk, scale=1.0 / math.sqrt(HEAD_DIM))
    return pl.pallas_call(
        kern,
        grid=(b, heads, s // bq),
        in_specs=[
            pl.BlockSpec((1, bq, HEAD_DIM), lambda bi, h, qi: (bi, qi, h)),
            pl.BlockSpec((1, s, HEAD_DIM), lambda bi, h, qi: (bi, 0, heads + h)),
            pl.BlockSpec((1, s, HEAD_DIM), lambda bi, h, qi: (bi, 0, 2 * heads + h)),
            pl.BlockSpec((2 * bk, bk + HEAD_DIM), lambda bi, h, qi: (0, 0)),
            pl.BlockSpec((1, HEAD_DIM), lambda bi, h, qi: (0, h)),
        ],
        out_specs=pl.BlockSpec((1, bq, HEAD_DIM), lambda bi, h, qi: (bi, qi, h)),
        out_shape=jax.ShapeDtypeStruct((b, s, heads * HEAD_DIM), BF16),
        compiler_params=_params(("arbitrary", "arbitrary", "arbitrary")),
        name="stickbreak_attention",
    )(qkv3, qkv3, qkv3, _suffix_matrix(bk), mix_gain)


def _outproj_kernel(ya_ref, yb_ref, yc_ref, x_ref, w_ref, g_ref, wr_ref, br_ref,
                    x1_ref, h2_ref, lg_ref, *, da, db):
    acc = jnp.dot(ya_ref[...], w_ref[0:da, :], preferred_element_type=F32)
    acc = acc + jnp.dot(yb_ref[...], w_ref[da:da + db, :], preferred_element_type=F32)
    acc = acc + jnp.dot(yc_ref[...], w_ref[da + db:, :], preferred_element_type=F32)
    x1 = x_ref[...] + acc
    x1_ref[...] = x1
    h2 = _rms(x1) * g_ref[...]
    h2_ref[...] = h2
    lg = lax.dot_general(wr_ref[...], h2.astype(BF16), (((1,), (1,)), ((), ())),
                         preferred_element_type=F32)
    lg_ref[...] = lg + br_ref[...]


def _out_projection(ya, yb, yc, x, w_bf16, gain, w_route_t, b_route):
    t, d = x.shape
    da, db, dc = ya.shape[1], yb.shape[1], yc.shape[1]
    tm = _tile(t, 256)
    kern = functools.partial(_outproj_kernel, da=da, db=db)
    row = lambda n: pl.BlockSpec((tm, n), lambda i: (i, 0))
    full = lambda shp: pl.BlockSpec(shp, lambda i: (0, 0))
    return pl.pallas_call(
        kern,
        grid=(t // tm,),
        in_specs=[row(da), row(db), row(dc), row(d), full((da + db + dc, d)), full((1, d)),
                  full((ROUTE_ROWS, d)), full((ROUTE_ROWS, 1))],
        out_specs=[row(d), row(d), pl.BlockSpec((ROUTE_ROWS, tm), lambda i: (0, i))],
        out_shape=[jax.ShapeDtypeStruct((t, d), F32), jax.ShapeDtypeStruct((t, d), F32),
                   jax.ShapeDtypeStruct((ROUTE_ROWS, t), F32)],
        compiler_params=_params(("arbitrary",)),
        name="out_projection",
    )(ya, yb, yc, x, w_bf16, gain, w_route_t, b_route)


def _route_kernel(lg_ref, tri_ref, ones_ref, ltri_ref, dest_ref, gate_ref, meta_ref,
                  cnt_scr, pst_scr, run_scr, *, n_experts, n_groups, row_block, tl):
    ph = pl.program_id(0)
    i = pl.program_id(1)
    epg = n_experts // n_groups

    @pl.when((ph == 0) & (i == 0))
    def _():
        cnt_scr[...] = jnp.zeros_like(cnt_scr)

    lg = lg_ref[...]
    grp = lg[n_experts:n_experts + n_groups, :]
    gmax = jnp.max(grp, axis=0, keepdims=True)
    giota = lax.broadcasted_iota(jnp.int32, grp.shape, 0)
    gidx = jnp.min(jnp.where(grp == gmax, giota, n_groups), axis=0, keepdims=True)
    p_grp = 1.0 / jnp.sum(jnp.exp(grp - gmax), axis=0, keepdims=True)

    sel = lg[0:epg, :]
    for g in range(1, n_groups):
        sel = jnp.where(gidx == g, lg[g * epg:(g + 1) * epg, :], sel)
    eiota = lax.broadcasted_iota(jnp.int32, sel.shape, 0)
    m1 = jnp.max(sel, axis=0, keepdims=True)
    i1 = jnp.min(jnp.where(sel == m1, eiota, epg), axis=0, keepdims=True)
    rest = jnp.where(eiota == i1, -jnp.inf, sel)
    m2 = jnp.max(rest, axis=0, keepdims=True)
    i2 = jnp.min(jnp.where(rest == m2, eiota, epg), axis=0, keepdims=True)
    e = jnp.exp(m2 - m1)
    den = 1.0 + e
    ex1 = gidx * epg + i1
    ex2 = gidx * epg + i2

    xiota = lax.broadcasted_iota(jnp.int32, (n_experts, tl), 0)
    hit1 = xiota == ex1
    hit2 = xiota == ex2
    onehot = (hit1 | hit2).astype(BF16)
    tile_cnt = jnp.dot(onehot, ones_ref[...], preferred_element_type=F32)

    @pl.when(ph == 0)
    def _():
        cnt_scr[...] += tile_cnt

    @pl.when((ph == 1) & (i == 0))
    def _():
        cnt = cnt_scr[...].astype(jnp.int32)
        nblk = (cnt + (row_block - 1)) // row_block
        pst = jnp.dot(ltri_ref[...], nblk.astype(F32).astype(BF16), preferred_element_type=F32)
        pst_scr[...] = pst
        run_scr[...] = jnp.zeros_like(run_scr)
        meta_ref[0:n_experts, :] = pst.astype(jnp.int32)
        meta_ref[n_experts:2 * n_experts, :] = cnt

    @pl.when(ph == 1)
    def _():
        rank = jnp.dot(onehot, tri_ref[...], preferred_element_type=F32)
        base = pst_scr[...] * float(row_block) + run_scr[...]
        slot = jnp.concatenate([base] * (tl // HEAD_DIM), axis=1) + rank
        d1 = jnp.sum(jnp.where(hit1, slot, 0.0), axis=0, keepdims=True)
        d2 = jnp.sum(jnp.where(hit2, slot, 0.0), axis=0, keepdims=True)
        dest_ref[0:1, :] = d1.astype(jnp.int32)
        dest_ref[1:2, :] = d2.astype(jnp.int32)
        gate_ref[0:1, :] = p_grp * (1.0 / den)
        gate_ref[1:2, :] = p_grp * (e / den)
        run_scr[...] += tile_cnt


def _route(logits_t, n_experts, n_groups, row_block):
    t = logits_t.shape[1]
    tl = _tile(t, 512)
    assert tl % HEAD_DIM == 0
    a = jnp.arange(tl)
    tri = (a[:, None] < a[None, :]).astype(BF16)
    ones = jnp.ones((tl, HEAD_DIM), BF16)
    ea = jnp.arange(n_experts)
    ltri = (ea[None, :] < ea[:, None]).astype(BF16)
    kern = functools.partial(_route_kernel, n_experts=n_experts, n_groups=n_groups,
                             row_block=row_block, tl=tl)
    full = lambda shp: pl.BlockSpec(shp, lambda p, i: (0, 0))
    return pl.pallas_call(
        kern,
        grid=(2, t // tl),
        in_specs=[pl.BlockSpec((ROUTE_ROWS, tl), lambda p, i: (0, i)),
                  full((tl, tl)), full((tl, HEAD_DIM)), full((n_experts, n_experts))],
        out_specs=[pl.BlockSpec((2, tl), lambda p, i: (0, i * p)),
                   pl.BlockSpec((2, tl), lambda p, i: (0, i * p)),
                   full((2 * n_experts, HEAD_DIM))],
        out_shape=[jax.ShapeDtypeStruct((2, t), jnp.int32), jax.ShapeDtypeStruct((2, t), F32),
                   jax.ShapeDtypeStruct((2 * n_experts, HEAD_DIM), jnp.int32)],
        scratch_shapes=[pltpu.VMEM((n_experts, HEAD_DIM), F32),
                        pltpu.VMEM((n_experts, HEAD_DIM), F32),
                        pltpu.VMEM((n_experts, HEAD_DIM), F32)],
        compiler_params=_params(("arbitrary", "arbitrary")),
        name="moe_route",
    )(logits_t, tri, ones, ltri)


def _tables_kernel(dest_ref, meta_ref, src_ref, bexp_ref, nval_ref, nused_ref, *,
                   t, n_experts, n_blocks, row_block):
    def zero(ri, _):
        src_ref[ri] = 0
        return 0

    lax.fori_loop(0, n_blocks * row_block, zero, 0)

    def fill(tok, _):
        src_ref[dest_ref[tok]] = tok
        src_ref[dest_ref[t + tok]] = t + tok
        return 0

    lax.fori_loop(0, t, fill, 0)

    def clear(bi, _):
        bexp_ref[bi] = n_experts - 1
        nval_ref[bi] = 0
        return 0

    lax.fori_loop(0, n_blocks, clear, 0)

    def per_expert(ex, used):
        first = meta_ref[ex]
        cnt = meta_ref[n_experts + ex]
        nblk = (cnt + (row_block - 1)) // row_block

        def per_block(bj, _):
            bexp_ref[first + bj] = ex
            nval_ref[first + bj] = jnp.minimum(cnt - bj * row_block, row_block)
            return 0

        lax.fori_loop(0, nblk, per_block, 0)
        return first + nblk

    nused_ref[0] = lax.fori_loop(0, n_experts, per_expert, 0)


def _tables(dest_flat, meta_flat, t, n_experts, n_blocks, row_block):
    kern = functools.partial(_tables_kernel, t=t, n_experts=n_experts, n_blocks=n_blocks,
                             row_block=row_block)
    smem = pl.BlockSpec(memory_space=pltpu.SMEM)
    return pl.pallas_call(
        kern,
        in_specs=[smem, smem],
        out_specs=[smem, smem, smem, smem],
        out_shape=[jax.ShapeDtypeStruct((n_blocks * row_block,), jnp.int32),
                   jax.ShapeDtypeStruct((n_blocks,), jnp.int32),
                   jax.ShapeDtypeStruct((n_blocks,), jnp.int32),
                   jax.ShapeDtypeStruct((1,), jnp.int32)],
        name="moe_tables",
    )(dest_flat, meta_flat)


def _expert_kernel(bexp_ref, nval_ref, nused_ref, src_ref, h2_hbm, wg_ref, wu_ref, wd_ref,
                   y_hbm, xbuf, ybuf, gsem, ssem, *, t, row_block, n_blocks):
    i = pl.program_id(0)
    n_used = nused_ref[0]
    slot = i % 2

    def gather_row(blk, s, r):
        src = src_ref[blk * row_block + r]
        tok = jnp.where(src >= t, src - t, src)
        return pltpu.make_async_copy(h2_hbm.at[pl.ds(tok, 1), :], xbuf.at[s, pl.ds(r, 1), :],
                                     gsem.at[s])

    def scatter_row(blk, s, r):
        src = src_ref[blk * row_block + r]
        return pltpu.make_async_copy(ybuf.at[s, pl.ds(r, 1), :], y_hbm.at[pl.ds(src, 1), :],
                                     ssem.at[s])

    def for_rows(blk, fn):
        def body(r, _):
            fn(r)
            return 0
        lax.fori_loop(0, nval_ref[blk], body, 0)

    @pl.when(i == 0)
    def _():
        xbuf[...] = jnp.zeros_like(xbuf)
        for_rows(0, lambda r: gather_row(0, 0, r).start())

    @pl.when(i + 1 < n_used)
    def _():
        for_rows(i + 1, lambda r: gather_row(i + 1, 1 - slot, r).start())

    @pl.when(i < n_used)
    def _():
        for_rows(i, lambda r: gather_row(i, slot, r).wait())

        @pl.when(i >= 2)
        def _():
            for_rows(i - 2, lambda r: scatter_row(i - 2, slot, r).wait())

        x = xbuf[slot].astype(BF16)
        gt = jnp.dot(x, wg_ref[0], preferred_element_type=F32)
        up = jnp.dot(x, wu_ref[0], preferred_element_type=F32)
        hid = (jax.nn.silu(gt) * up).astype(BF16)
        ybuf[slot] = jnp.dot(hid, wd_ref[0], preferred_element_type=F32)
        for_rows(i, lambda r: scatter_row(i, slot, r).start())

    @pl.when(i == n_blocks - 1)
    def _():
        last = n_used - 1
        for_rows(last, lambda r: scatter_row(last, last % 2, r).wait())

        @pl.when(n_used >= 2)
        def _():
            for_rows(last - 1, lambda r: scatter_row(last - 1, (last - 1) % 2, r).wait())


def _experts(h2, w_gate, w_up, w_down, bexp, nval, nused, src, row_block):
    t, d = h2.shape
    n_blocks = bexp.shape[0]
    de = w_gate.shape[2]
    kern = functools.partial(_expert_kernel, t=t, row_block=row_block, n_blocks=n_blocks)
    grid_spec = pltpu.PrefetchScalarGridSpec(
        num_scalar_prefetch=4,
        grid=(n_blocks,),
        in_specs=[
            pl.BlockSpec(memory_space=pl.ANY),
            pl.BlockSpec((1, d, de), lambda i, be, nv, nu, sr: (be[i], 0, 0)),
            pl.BlockSpec((1, d, de), lambda i, be, nv, nu, sr: (be[i], 0, 0)),
            pl.BlockSpec((1, de, d), lambda i, be, nv, nu, sr: (be[i], 0, 0)),
        ],
        out_specs=pl.BlockSpec(memory_space=pl.ANY),
        scratch_shapes=[
            pltpu.VMEM((2, row_block, d), F32),
            pltpu.VMEM((2, row_block, d), F32),
            pltpu.SemaphoreType.DMA((2,)),
            pltpu.SemaphoreType.DMA((2,)),
        ],
    )
    return pl.pallas_call(
        kern,
        grid_spec=grid_spec,
        out_shape=jax.ShapeDtypeStruct((2 * t, d), F32),
        compiler_params=_params(("arbitrary",)),
        name="moe_experts",
    )(bexp, nval, nused, src, h2, w_gate, w_up, w_down)


def _combine_kernel(x_ref, y0_ref, y1_ref, g_ref, o_ref):
    g = g_ref[...]
    o_ref[...] = x_ref[...] + (y0_ref[...] * g[:, 0:1] + y1_ref[...] * g[:, 1:2])


def _combine(x1, y_tok, gates_t):
    t, d = x1.shape
    tm = _tile(t, 512)
    nb = t // tm
    return pl.pallas_call(
        _combine_kernel,
        grid=(nb,),
        in_specs=[pl.BlockSpec((tm, d), lambda i: (i, 0)),
                  pl.BlockSpec((tm, d), lambda i: (i, 0)),
                  pl.BlockSpec((tm, d), lambda i: (i + nb, 0)),
                  pl.BlockSpec((tm, 2), lambda i: (i, 0))],
        out_specs=pl.BlockSpec((tm, d), lambda i: (i, 0)),
        out_shape=jax.ShapeDtypeStruct((t, d), F32),
        compiler_params=_params(("arbitrary",)),
        name="moe_combine",
    )(x1, y_tok, y_tok, gates_t)


MOE_ROW_BLOCK = 256


def kernel(x, norm_mix, w_in, conv_a_w, conv_a_b, lru_wa, lru_ba, lru_wx, lru_bx, lru_lam,
           conv_b_w, q_norm, k_norm, mix_norm, w_out, norm_ffn, w_router_group, b_router_group,
           w_router_expert, b_router_expert, w_gate, w_up, w_down):
    bsz, s, d = x.shape
    t = bsz * s
    depth = w_in.shape[0]
    lru_heads = lru_wa.shape[1]
    d_lru = lru_heads * HEAD_DIM
    d_conv = conv_b_w.shape[-1]
    conv_groups = d_conv // HEAD_DIM
    d_sb = (w_in.shape[-1] - 2 * d_lru - 3 * d_conv) // 3
    sb_heads = d_sb // HEAD_DIM
    n_groups = w_router_group.shape[-1]
    n_experts = w_router_expert.shape[-1]
    assert n_experts // n_groups == EXPERTS_PER_GROUP and n_experts + n_groups <= ROUTE_ROWS
    row_block = MOE_ROW_BLOCK
    n_blocks = -(-(2 * t) // row_block) + n_experts

    xt = x.reshape(t, d)
    for l in range(depth):
        row = lambda v: v[l].reshape(1, -1)
        mix, qkv = _in_projection(xt, row(norm_mix), w_in[l].astype(BF16), row(q_norm), row(k_norm), d_sb)
        mix3 = mix.reshape(bsz, s, -1)
        qkv3 = qkv.reshape(bsz, s, -1)
        mg = row(mix_norm)
        w_gates = jnp.concatenate([lru_wa[l], lru_wx[l]], axis=-1).astype(BF16)
        ya = _mixer_a(mix3, conv_a_w[l], row(conv_a_b), w_gates, row(lru_ba), row(lru_bx),
                      row(lru_lam), mg[:, :d_lru], lru_heads)
        yb = _mixer_b(mix3, conv_b_w[l], mg[:, d_lru:d_lru + d_conv], 2 * lru_heads, conv_groups)
        yc = _attention(qkv3, mg[:, d_lru + d_conv:], sb_heads)

        pad = jnp.zeros((d, ROUTE_ROWS - n_experts - n_groups), F32)
        w_route_t = jnp.concatenate([w_router_expert[l], w_router_group[l], pad], axis=1).T.astype(BF16)
        b_route = jnp.concatenate([b_router_expert[l], b_router_group[l],
                                   jnp.zeros((ROUTE_ROWS - n_experts - n_groups,), F32)]).reshape(-1, 1)
        x1, h2, logits_t = _out_projection(ya.reshape(t, -1), yb.reshape(t, -1), yc.reshape(t, -1), xt,
                                           w_out[l].astype(BF16), row(norm_ffn), w_route_t, b_route)

        dest, gates, meta = _route(logits_t, n_experts, n_groups, row_block)
        src, bexp, nval, nused = _tables(dest.reshape(-1), meta[:, 0], t, n_experts, n_blocks, row_block)
        y_tok = _experts(h2, w_gate[l].astype(BF16), w_up[l].astype(BF16), w_down[l].astype(BF16),
                         bexp, nval, nused, src, row_block)
        xt = _combine(x1, y_tok, gates.T)
    return xt.reshape(bsz, s, d)
```

```python
import functools
import math

import jax
import jax.numpy as jnp
from jax import lax
from jax.experimental import pallas as pl
from jax.experimental.pallas import tpu as pltpu

EPS = 1e-6
LRU_C = 8.0
HEAD_DIM = 128
SUBLANES = 8
EXPERTS_PER_GROUP = 8
ROUTE_ROWS = 48
VMEM_LIMIT = 56 * 1024 * 1024

F32 = jnp.float32
BF16 = jnp.bfloat16


def _tile(n, pref):
    if n <= pref:
        return n
    t = pref
    while n % t:
        t -= SUBLANES
    assert t > 0
    return t


def _params(sem):
    return pltpu.CompilerParams(dimension_semantics=sem, vmem_limit_bytes=VMEM_LIMIT)


def _rms(v):
    return v * lax.rsqrt(jnp.mean(v * v, axis=-1, keepdims=True) + EPS)


def _inproj_kernel(x_ref, g_ref, w_ref, qn_ref, kn_ref, mix_ref, qkv_ref, h_scr, *, n_mix, heads):
    j = pl.program_id(1)

    @pl.when(j == 0)
    def _():
        h_scr[...] = (_rms(x_ref[...]) * g_ref[...]).astype(BF16)

    acc = jnp.dot(h_scr[...], w_ref[...], preferred_element_type=F32)

    @pl.when(j < n_mix)
    def _():
        mix_ref[...] = acc

    def head_norm(gain_ref):
        for h in range(heads):
            sl = slice(h * HEAD_DIM, (h + 1) * HEAD_DIM)
            qkv_ref[:, sl] = (_rms(acc[:, sl]) * gain_ref[...]).astype(BF16)

    @pl.when(j == n_mix)
    def _():
        head_norm(qn_ref)

    @pl.when(j == n_mix + 1)
    def _():
        head_norm(kn_ref)

    @pl.when(j == n_mix + 2)
    def _():
        qkv_ref[...] = acc.astype(BF16)


def _in_projection(x, gain, w_bf16, q_gain, k_gain, d_sb):
    t, d = x.shape
    d_in = w_bf16.shape[1]
    tn = d_sb
    assert d_in % tn == 0 and (d_in - 3 * d_sb) % tn == 0
    n_mix = (d_in - 3 * d_sb) // tn
    tm = _tile(t, 1024)
    kern = functools.partial(_inproj_kernel, n_mix=n_mix, heads=d_sb // HEAD_DIM)
    return pl.pallas_call(
        kern,
        grid=(t // tm, d_in // tn),
        in_specs=[
            pl.BlockSpec((tm, d), lambda i, j: (i, 0)),
            pl.BlockSpec((1, d), lambda i, j: (0, 0)),
            pl.BlockSpec((d, tn), lambda i, j: (0, j)),
            pl.BlockSpec((1, HEAD_DIM), lambda i, j: (0, 0)),
            pl.BlockSpec((1, HEAD_DIM), lambda i, j: (0, 0)),
        ],
        out_specs=[
            pl.BlockSpec((tm, tn), lambda i, j: (i, jnp.minimum(j, n_mix - 1))),
            pl.BlockSpec((tm, tn), lambda i, j: (i, jnp.maximum(j - n_mix, 0))),
        ],
        out_shape=[
            jax.ShapeDtypeStruct((t, n_mix * tn), F32),
            jax.ShapeDtypeStruct((t, 3 * d_sb), BF16),
        ],
        scratch_shapes=[pltpu.VMEM((tm, d), BF16)],
        compiler_params=_params(("arbitrary", "arbitrary")),
        name="in_projection",
    )(x, gain, w_bf16, q_gain, k_gain)


def _softplus(v):
    return jnp.maximum(v, 0.0) + jnp.log1p(jnp.exp(-jnp.abs(v)))


def _mixer_a_kernel(xa_ref, ga_ref, cw_ref, cb_ref, wg_ref, ba_ref, bx_ref, lam_ref, mn_ref,
                    out_ref, ext_scr, a_scr, u_scr, hc_scr, *, ts, taps):
    si = pl.program_id(2)

    @pl.when(si == 0)
    def _():
        ext_scr[0:SUBLANES, :] = jnp.zeros((SUBLANES, HEAD_DIM), F32)
        hc_scr[...] = jnp.zeros((SUBLANES, HEAD_DIM), F32)

    ext_scr[SUBLANES:SUBLANES + ts, :] = xa_ref[0]
    cw = cw_ref[...]
    xc = cb_ref[...]
    for k in range(taps):
        lo = SUBLANES - (taps - 1) + k
        xc = xc + cw[k:k + 1, :] * ext_scr[lo:lo + ts, :]
    ext_scr[0:SUBLANES, :] = ext_scr[ts:ts + SUBLANES, :]

    gates = jnp.dot(xc.astype(BF16), wg_ref[0], preferred_element_type=F32)
    r = jax.nn.sigmoid(gates[:, :HEAD_DIM] + ba_ref[...])
    i = jax.nn.sigmoid(gates[:, HEAD_DIM:] + bx_ref[...])
    log_a = (-LRU_C) * r * _softplus(-lam_ref[...])
    a = jnp.exp(log_a)
    u = jnp.sqrt(-jnp.tanh(log_a) * (a * a + 1.0)) * (i * xc)

    sub = lax.broadcasted_iota(jnp.int32, (ts, HEAD_DIM), 0) & (SUBLANES - 1)
    d = 1
    while d < SUBLANES:
        keep = sub >= d
        a_sh = jnp.where(keep, pltpu.roll(a, d, 0), 1.0)
        u_sh = jnp.where(keep, pltpu.roll(u, d, 0), 0.0)
        u = a * u_sh + u
        a = a * a_sh
        d *= 2
    a_scr[...] = a
    u_scr[...] = u

    def carry_step(c, hb):
        off = pl.multiple_of(c * SUBLANES, SUBLANES)
        h = u_scr[pl.ds(off, SUBLANES), :] + a_scr[pl.ds(off, SUBLANES), :] * hb
        u_scr[pl.ds(off, SUBLANES), :] = h
        return jnp.broadcast_to(h[SUBLANES - 1:SUBLANES, :], (SUBLANES, HEAD_DIM))

    hc_scr[...] = lax.fori_loop(0, ts // SUBLANES, carry_step, hc_scr[...], unroll=8)

    y = u_scr[...] * jax.nn.gelu(ga_ref[0])
    out_ref[0] = (_rms(y) * mn_ref[...]).astype(BF16)


def _mixer_a(mix3, conv_w, conv_b, w_gates, b_a, b_x, lam, mix_gain, heads):
    b, s, _ = mix3.shape
    ts = _tile(s, 512)
    taps = conv_w.shape[0]
    assert taps - 1 <= SUBLANES and ts % SUBLANES == 0
    kern = functools.partial(_mixer_a_kernel, ts=ts, taps=taps)
    vec = lambda: pl.BlockSpec((1, HEAD_DIM), lambda bi, h, si: (0, h))
    return pl.pallas_call(
        kern,
        grid=(b, heads, s // ts),
        in_specs=[
            pl.BlockSpec((1, ts, HEAD_DIM), lambda bi, h, si: (bi, si, h)),
            pl.BlockSpec((1, ts, HEAD_DIM), lambda bi, h, si: (bi, si, heads + h)),
            pl.BlockSpec((taps, HEAD_DIM), lambda bi, h, si: (0, h)),
            vec(),
            pl.BlockSpec((1, HEAD_DIM, 2 * HEAD_DIM), lambda bi, h, si: (h, 0, 0)),
            vec(), vec(), vec(), vec(),
        ],
        out_specs=pl.BlockSpec((1, ts, HEAD_DIM), lambda bi, h, si: (bi, si, h)),
        out_shape=jax.ShapeDtypeStruct((b, s, heads * HEAD_DIM), BF16),
        scratch_shapes=[
            pltpu.VMEM((ts + SUBLANES, HEAD_DIM), F32),
            pltpu.VMEM((ts, HEAD_DIM), F32),
            pltpu.VMEM((ts, HEAD_DIM), F32),
            pltpu.VMEM((SUBLANES, HEAD_DIM), F32),
        ],
        compiler_params=_params(("arbitrary", "arbitrary", "arbitrary")),
        name="mixer_rglru",
    )(mix3, mix3, conv_w, conv_b, w_gates, b_a, b_x, lam, mix_gain)


def _mixer_b_kernel(bg_ref, cg_ref, xb_ref, cw_ref, mn_ref, out_ref, ext_scr, *, ts, taps):
    si = pl.program_id(2)

    @pl.when(si == 0)
    def _():
        ext_scr[0:SUBLANES, :] = jnp.zeros((SUBLANES, HEAD_DIM), F32)

    ext_scr[SUBLANES:SUBLANES + ts, :] = cg_ref[0] * xb_ref[0]
    cw = cw_ref[...]
    conv = jnp.zeros((ts, HEAD_DIM), F32)
    for k in range(taps):
        lo = SUBLANES - (taps - 1) + k
        conv = conv + cw[k:k + 1, :] * ext_scr[lo:lo + ts, :]
    ext_scr[0:SUBLANES, :] = ext_scr[ts:ts + SUBLANES, :]
    y = bg_ref[0] * conv
    out_ref[0] = (_rms(y) * mn_ref[...]).astype(BF16)


def _mixer_b(mix3, conv_w, mix_gain, col0, groups):
    b, s, _ = mix3.shape
    ts = _tile(s, 512)
    taps = conv_w.shape[0]
    assert taps - 1 <= SUBLANES and ts % SUBLANES == 0
    kern = functools.partial(_mixer_b_kernel, ts=ts, taps=taps)
    blk = lambda off: pl.BlockSpec((1, ts, HEAD_DIM), lambda bi, g, si: (bi, si, off + g))
    return pl.pallas_call(
        kern,
        grid=(b, groups, s // ts),
        in_specs=[
            blk(col0), blk(col0 + groups), blk(col0 + 2 * groups),
            pl.BlockSpec((taps, HEAD_DIM), lambda bi, g, si: (0, g)),
            pl.BlockSpec((1, HEAD_DIM), lambda bi, g, si: (0, g)),
        ],
        out_specs=pl.BlockSpec((1, ts, HEAD_DIM), lambda bi, g, si: (bi, si, g)),
        out_shape=jax.ShapeDtypeStruct((b, s, groups * HEAD_DIM), BF16),
        scratch_shapes=[pltpu.VMEM((ts + SUBLANES, HEAD_DIM), F32)],
        compiler_params=_params(("arbitrary", "arbitrary", "arbitrary")),
        name="mixer_gated_conv",
    )(mix3, mix3, mix3, conv_w, mix_gain)


def _attn_kernel(q_ref, k_ref, v_ref, m_ref, mn_ref, o_ref, *, bq, bk, scale):
    qi = pl.program_id(2)
    q = q_ref[0]
    n_sub = bq // bk

    def chunk(c0, run, diagonal):
        kc = k_ref[0, pl.ds(c0, bq), :]
        vc = v_ref[0, pl.ds(c0, bq), :]
        z = lax.dot_general(q, kc, (((1,), (1,)), ((), ())), preferred_element_type=F32) * scale
        ls = jnp.minimum(z, 0.0) - jnp.log(1.0 + jnp.exp(-jnp.abs(z)))
        lk = ls - z
        if diagonal:
            mask = (lax.broadcasted_iota(jnp.int32, (bq, bq), 1)
                    < lax.broadcasted_iota(jnp.int32, (bq, bq), 0))
            lk = jnp.where(mask, lk, 0.0)
        ws = [None] * n_sub
        for k in reversed(range(n_sub)):
            sl = slice(k * bk, (k + 1) * bk)
            hi = lk[:, sl].astype(BF16)
            lo = (lk[:, sl] - hi.astype(F32)).astype(BF16)
            sums = jnp.dot(jnp.concatenate([hi, lo], axis=1), m_ref[...], preferred_element_type=F32)
            w = jnp.exp(ls[:, sl] + sums[:, :bk] + run)
            if diagonal:
                w = jnp.where(mask[:, sl], w, 0.0)
            ws[k] = w.astype(BF16)
            run = run + sums[:, bk:]
        pv = jnp.dot(jnp.concatenate(ws, axis=1), vc, preferred_element_type=F32)
        return pv, run

    acc, run = chunk(pl.multiple_of(qi * bq, bq), jnp.zeros((bq, HEAD_DIM), F32), True)

    def earlier_chunk(j, carry):
        acc, run = carry
        pv, run = chunk(pl.multiple_of((qi - j) * bq, bq), run, False)
        return acc + pv, run

    acc, _ = lax.fori_loop(1, qi + 1, earlier_chunk, (acc, run))
    o_ref[0] = (_rms(acc) * mn_ref[...]).astype(BF16)


def _suffix_matrix(bk):
    j = jnp.arange(bk)[:, None]
    s = jnp.arange(bk)[None, :]
    half = jnp.concatenate([(j > s).astype(BF16), jnp.ones((bk, HEAD_DIM), BF16)], axis=1)
    return jnp.concatenate([half, half], axis=0)


def _attention(qkv3, mix_gain, heads):
    b, s, _ = qkv3.shape
    bk = HEAD_DIM
    bq = _tile(s, 512)
    assert bq % bk == 0 and s % bq == 0
    kern = functools.partial(_attn_kernel, bq=bq, bk=bk, scale=1.0 / math.sqrt(HEAD_DIM))
    return pl.pallas_call(
        kern,
        grid=(b, heads, s // bq),
        in_specs=[
            pl.BlockSpec((1, bq, HEAD_DIM), lambda bi, h, qi: (bi, qi, h)),
            pl.BlockSpec((1, s, HEAD_DIM), lambda bi, h, qi: (bi, 0, heads + h)),
            pl.BlockSpec((1, s, HEAD_DIM), lambda bi, h, qi: (bi, 0, 2 * heads + h)),
            pl.BlockSpec((2 * bk, bk + HEAD_DIM), lambda bi, h, qi: (0, 0)),
            pl.BlockSpec((1, HEAD_DIM), lambda bi, h, qi: (0, h)),
        ],
        out_specs=pl.BlockSpec((1, bq, HEAD_DIM), lambda bi, h, qi: (bi, qi, h)),
        out_shape=jax.ShapeDtypeStruct((b, s, heads * HEAD_DIM), BF16),
        compiler_params=_params(("arbitrary", "arbitrary", "arbitrary")),
        name="stickbreak_attention",
    )(qkv3, qkv3, qkv3, _suffix_matrix(bk), mix_gain)


def _outproj_kernel(ya_ref, yb_ref, yc_ref, x_ref, w_ref, g_ref, wr_ref, br_ref,
                    x1_ref, h2_ref, lg_ref, *, da, db):
    acc = jnp.dot(ya_ref[...], w_ref[0:da, :], preferred_element_type=F32)
    acc = acc + jnp.dot(yb_ref[...], w_ref[da:da + db, :], preferred_element_type=F32)
    acc = acc + jnp.dot(yc_ref[...], w_ref[da + db:, :], preferred_element_type=F32)
    x1 = x_ref[...] + acc
    x1_ref[...] = x1
    h2 = _rms(x1) * g_ref[...]
    h2_ref[...] = h2
    lg = lax.dot_general(wr_ref[...], h2.astype(BF16), (((1,), (1,)), ((), ())),
                         preferred_element_type=F32)
    lg_ref[...] = lg + br_ref[...]


def _out_projection(ya, yb, yc, x, w_bf16, gain, w_route_t, b_route):
    t, d = x.shape
    da, db, dc = ya.shape[1], yb.shape[1], yc.shape[1]
    tm = _tile(t, 256)
    kern = functools.partial(_outproj_kernel, da=da, db=db)
    row = lambda n: pl.BlockSpec((tm, n), lambda i: (i, 0))
    full = lambda shp: pl.BlockSpec(shp, lambda i: (0, 0))
    return pl.pallas_call(
        kern,
        grid=(t // tm,),
        in_specs=[row(da), row(db), row(dc), row(d), full((da + db + dc, d)), full((1, d)),
                  full((ROUTE_ROWS, d)), full((ROUTE_ROWS, 1))],
        out_specs=[row(d), row(d), pl.BlockSpec((ROUTE_ROWS, tm), lambda i: (0, i))],
        out_shape=[jax.ShapeDtypeStruct((t, d), F32), jax.ShapeDtypeStruct((t, d), F32),
                   jax.ShapeDtypeStruct((ROUTE_ROWS, t), F32)],
        compiler_params=_params(("arbitrary",)),
        name="out_projection",
    )(ya, yb, yc, x, w_bf16, gain, w_route_t, b_route)


def _route_kernel(lg_ref, tri_ref, ones_ref, ltri_ref, dest_ref, gate_ref, meta_ref,
                  cnt_scr, pst_scr, run_scr, *, n_experts, n_groups, row_block, tl):
    ph = pl.program_id(0)
    i = pl.program_id(1)
    epg = n_experts // n_groups

    @pl.when((ph == 0) & (i == 0))
    def _():
        cnt_scr[...] = jnp.zeros_like(cnt_scr)

    lg = lg_ref[...]
    grp = lg[n_experts:n_experts + n_groups, :]
    gmax = jnp.max(grp, axis=0, keepdims=True)
    giota = lax.broadcasted_iota(jnp.int32, grp.shape, 0)
    gidx = jnp.min(jnp.where(grp == gmax, giota, n_groups), axis=0, keepdims=True)
    p_grp = 1.0 / jnp.sum(jnp.exp(grp - gmax), axis=0, keepdims=True)

    sel = lg[0:epg, :]
    for g in range(1, n_groups):
        sel = jnp.where(gidx == g, lg[g * epg:(g + 1) * epg, :], sel)
    eiota = lax.broadcasted_iota(jnp.int32, sel.shape, 0)
    m1 = jnp.max(sel, axis=0, keepdims=True)
    i1 = jnp.min(jnp.where(sel == m1, eiota, epg), axis=0, keepdims=True)
    rest = jnp.where(eiota == i1, -jnp.inf, sel)
    m2 = jnp.max(rest, axis=0, keepdims=True)
    i2 = jnp.min(jnp.where(rest == m2, eiota, epg), axis=0, keepdims=True)
    e = jnp.exp(m2 - m1)
    den = 1.0 + e
    ex1 = gidx * epg + i1
    ex2 = gidx * epg + i2

    xiota = lax.broadcasted_iota(jnp.int32, (n_experts, tl), 0)
    hit1 = xiota == ex1
    hit2 = xiota == ex2
    onehot = (hit1 | hit2).astype(BF16)
    tile_cnt = jnp.dot(onehot, ones_ref[...], preferred_element_type=F32)

    @pl.when(ph == 0)
    def _():
        cnt_scr[...] += tile_cnt

    @pl.when((ph == 1) & (i == 0))
    def _():
        cnt = cnt_scr[...].astype(jnp.int32)
        nblk = (cnt + (row_block - 1)) // row_block
        pst = jnp.dot(ltri_ref[...], nblk.astype(F32).astype(BF16), preferred_element_type=F32)
        pst_scr[...] = pst
        run_scr[...] = jnp.zeros_like(run_scr)
        meta_ref[0:n_experts, :] = pst.astype(jnp.int32)
        meta_ref[n_experts:2 * n_experts, :] = cnt

    @pl.when(ph == 1)
    def _():
        rank = jnp.dot(onehot, tri_ref[...], preferred_element_type=F32)
        base = pst_scr[...] * float(row_block) + run_scr[...]
        slot = jnp.concatenate([base] * (tl // HEAD_DIM), axis=1) + rank
        d1 = jnp.sum(jnp.where(hit1, slot, 0.0), axis=0, keepdims=True)
        d2 = jnp.sum(jnp.where(hit2, slot, 0.0), axis=0, keepdims=True)
        dest_ref[0:1, :] = d1.astype(jnp.int32)
        dest_ref[1:2, :] = d2.astype(jnp.int32)
        gate_ref[0:1, :] = p_grp * (1.0 / den)
        gate_ref[1:2, :] = p_grp * (e / den)
        run_scr[...] += tile_cnt


def _route(logits_t, n_experts, n_groups, row_block):
    t = logits_t.shape[1]
    tl = _tile(t, 512)
    assert tl % HEAD_DIM == 0
    a = jnp.arange(tl)
    tri = (a[:, None] < a[None, :]).astype(BF16)
    ones = jnp.ones((tl, HEAD_DIM), BF16)
    ea = jnp.arange(n_experts)
    ltri = (ea[None, :] < ea[:, None]).astype(BF16)
    kern = functools.partial(_route_kernel, n_experts=n_experts, n_groups=n_groups,
                             row_block=row_block, tl=tl)
    full = lambda shp: pl.BlockSpec(shp, lambda p, i: (0, 0))
    return pl.pallas_call(
        kern,
        grid=(2, t // tl),
        in_specs=[pl.BlockSpec((ROUTE_ROWS, tl), lambda p, i: (0, i)),
                  full((tl, tl)), full((tl, HEAD_DIM)), full((n_experts, n_experts))],
        out_specs=[pl.BlockSpec((2, tl), lambda p, i: (0, i * p)),
                   pl.BlockSpec((2, tl), lambda p, i: (0, i * p)),
                   full((2 * n_experts, HEAD_DIM))],
        out_shape=[jax.ShapeDtypeStruct((2, t), jnp.int32), jax.ShapeDtypeStruct((2, t), F32),
                   jax.ShapeDtypeStruct((2 * n_experts, HEAD_DIM), jnp.int32)],
        scratch_shapes=[pltpu.VMEM((n_experts, HEAD_DIM), F32),
                        pltpu.VMEM((n_experts, HEAD_DIM), F32),
                        pltpu.VMEM((n_experts, HEAD_DIM), F32)],
        compiler_params=_params(("arbitrary", "arbitrary")),
        name="moe_route",
    )(logits_t, tri, ones, ltri)


PAD_ROW = -1


def _tables_kernel(dest_ref, meta_ref, src_ref, bexp_ref, nused_ref, *,
                   t, n_experts, n_blocks, row_block):
    def mark_pad(ri, _):
        src_ref[ri] = PAD_ROW
        return 0

    lax.fori_loop(0, n_blocks * row_block, mark_pad, 0, unroll=16)

    def fill(tok, _):
        src_ref[dest_ref[tok]] = tok
        src_ref[dest_ref[t + tok]] = t + tok
        return 0

    lax.fori_loop(0, t, fill, 0, unroll=8)

    def clear(bi, _):
        bexp_ref[bi] = n_experts - 1
        return 0

    lax.fori_loop(0, n_blocks, clear, 0)

    def per_expert(ex, used):
        first = meta_ref[ex]
        nblk = (meta_ref[n_experts + ex] + (row_block - 1)) // row_block

        def per_block(bj, _):
            bexp_ref[first + bj] = ex
            return 0

        lax.fori_loop(0, nblk, per_block, 0)
        return first + nblk

    nused_ref[0] = lax.fori_loop(0, n_experts, per_expert, 0)


def _tables(dest_flat, meta_flat, t, n_experts, n_blocks, row_block):
    kern = functools.partial(_tables_kernel, t=t, n_experts=n_experts, n_blocks=n_blocks,
                             row_block=row_block)
    smem = pl.BlockSpec(memory_space=pltpu.SMEM)
    return pl.pallas_call(
        kern,
        in_specs=[smem, smem],
        out_specs=[smem, smem, smem],
        out_shape=[jax.ShapeDtypeStruct((n_blocks * row_block,), jnp.int32),
                   jax.ShapeDtypeStruct((n_blocks,), jnp.int32),
                   jax.ShapeDtypeStruct((1,), jnp.int32)],
        name="moe_tables",
    )(dest_flat, meta_flat)


def _expert_kernel(bexp_ref, nused_ref, src_ref, h2_hbm, wg_ref, wu_ref, wd_ref,
                   y_hbm, xbuf, ybuf, gsem, ssem, *, t, row_block, n_blocks):
    i = pl.program_id(0)
    n_used = nused_ref[0]
    slot = i % 2

    def start_gather(blk, s):
        def body(r, _):
            src = src_ref[blk * row_block + r]
            if t & (t - 1) == 0:
                tok = src & (t - 1)
            else:
                tok = jnp.where(src < 0, 0, jnp.where(src >= t, src - t, src))
            pltpu.make_async_copy(h2_hbm.at[pl.ds(tok, 1), :], xbuf.at[s, pl.ds(r, 1), :],
                                  gsem.at[s]).start()
            return 0
        lax.fori_loop(0, row_block, body, 0, unroll=8)

    def start_scatter(blk, s):
        def body(r, _):
            src = src_ref[blk * row_block + r]
            dst = jnp.where(src < 0, 2 * t + s * row_block + r, src)
            pltpu.make_async_copy(ybuf.at[s, pl.ds(r, 1), :], y_hbm.at[pl.ds(dst, 1), :],
                                  ssem.at[s]).start()
            return 0
        lax.fori_loop(0, row_block, body, 0, unroll=8)

    def wait_gather(s):
        pltpu.make_async_copy(xbuf.at[s], xbuf.at[s], gsem.at[s]).wait()

    def wait_scatter(s):
        pltpu.make_async_copy(ybuf.at[s], ybuf.at[s], ssem.at[s]).wait()

    @pl.when(i == 0)
    def _():
        ybuf[...] = jnp.zeros_like(ybuf)
        for s in range(2):
            spare = y_hbm.at[pl.ds(2 * t + s * row_block, row_block), :]
            pltpu.make_async_copy(ybuf.at[s], spare, ssem.at[s]).start()
        for s in range(2):
            wait_scatter(s)
        start_gather(0, 0)

    @pl.when(i + 1 < n_used)
    def _():
        start_gather(i + 1, 1 - slot)

    @pl.when(i < n_used)
    def _():
        wait_gather(slot)

        @pl.when(i >= 2)
        def _():
            wait_scatter(slot)

        x = xbuf[slot].astype(BF16)
        gt = jnp.dot(x, wg_ref[0], preferred_element_type=F32)
        up = jnp.dot(x, wu_ref[0], preferred_element_type=F32)
        hid = (jax.nn.silu(gt) * up).astype(BF16)
        ybuf[slot] = jnp.dot(hid, wd_ref[0], preferred_element_type=F32)
        start_scatter(i, slot)

    @pl.when(i == n_blocks - 1)
    def _():
        wait_scatter((n_used - 1) % 2)

        @pl.when(n_used >= 2)
        def _():
            wait_scatter(n_used % 2)


def _experts(h2, w_gate, w_up, w_down, bexp, nused, src, row_block):
    t, d = h2.shape
    n_blocks = bexp.shape[0]
    de = w_gate.shape[2]
    kern = functools.partial(_expert_kernel, t=t, row_block=row_block, n_blocks=n_blocks)
    grid_spec = pltpu.PrefetchScalarGridSpec(
        num_scalar_prefetch=3,
        grid=(n_blocks,),
        in_specs=[
            pl.BlockSpec(memory_space=pl.ANY),
            pl.BlockSpec((1, d, de), lambda i, be, nu, sr: (be[i], 0, 0)),
            pl.BlockSpec((1, d, de), lambda i, be, nu, sr: (be[i], 0, 0)),
            pl.BlockSpec((1, de, d), lambda i, be, nu, sr: (be[i], 0, 0)),
        ],
        out_specs=pl.BlockSpec(memory_space=pl.ANY),
        scratch_shapes=[
            pltpu.VMEM((2, row_block, d), F32),
            pltpu.VMEM((2, row_block, d), F32),
            pltpu.SemaphoreType.DMA((2,)),
            pltpu.SemaphoreType.DMA((2,)),
        ],
    )
    return pl.pallas_call(
        kern,
        grid_spec=grid_spec,
        out_shape=jax.ShapeDtypeStruct((2 * t + 2 * row_block, d), F32),
        compiler_params=_params(("arbitrary",)),
        name="moe_experts",
    )(bexp, nused, src, h2, w_gate, w_up, w_down)


def _combine_kernel(x_ref, y0_ref, y1_ref, g_ref, o_ref):
    g = g_ref[...]
    o_ref[...] = x_ref[...] + (y0_ref[...] * g[:, 0:1] + y1_ref[...] * g[:, 1:2])


def _combine(x1, y_tok, gates_t):
    t, d = x1.shape
    tm = _tile(t, 512)
    nb = t // tm
    return pl.pallas_call(
        _combine_kernel,
        grid=(nb,),
        in_specs=[pl.BlockSpec((tm, d), lambda i: (i, 0)),
                  pl.BlockSpec((tm, d), lambda i: (i, 0)),
                  pl.BlockSpec((tm, d), lambda i: (i + nb, 0)),
                  pl.BlockSpec((tm, 2), lambda i: (i, 0))],
        out_specs=pl.BlockSpec((tm, d), lambda i: (i, 0)),
        out_shape=jax.ShapeDtypeStruct((t, d), F32),
        compiler_params=_params(("arbitrary",)),
        name="moe_combine",
    )(x1, y_tok, y_tok, gates_t)


MOE_ROW_BLOCK = 256


def kernel(x, norm_mix, w_in, conv_a_w, conv_a_b, lru_wa, lru_ba, lru_wx, lru_bx, lru_lam,
           conv_b_w, q_norm, k_norm, mix_norm, w_out, norm_ffn, w_router_group, b_router_group,
           w_router_expert, b_router_expert, w_gate, w_up, w_down):
    bsz, s, d = x.shape
    t = bsz * s
    depth = w_in.shape[0]
    lru_heads = lru_wa.shape[1]
    d_lru = lru_heads * HEAD_DIM
    d_conv = conv_b_w.shape[-1]
    conv_groups = d_conv // HEAD_DIM
    d_sb = (w_in.shape[-1] - 2 * d_lru - 3 * d_conv) // 3
    sb_heads = d_sb // HEAD_DIM
    n_groups = w_router_group.shape[-1]
    n_experts = w_router_expert.shape[-1]
    assert n_experts // n_groups == EXPERTS_PER_GROUP and n_experts + n_groups <= ROUTE_ROWS
    row_block = MOE_ROW_BLOCK
    n_blocks = -(-(2 * t) // row_block) + n_experts

    xt = x.reshape(t, d)
    for l in range(depth):
        row = lambda v: v[l].reshape(1, -1)
        mix, qkv = _in_projection(xt, row(norm_mix), w_in[l].astype(BF16), row(q_norm), row(k_norm), d_sb)
        mix3 = mix.reshape(bsz, s, -1)
        qkv3 = qkv.reshape(bsz, s, -1)
        mg = row(mix_norm)
        w_gates = jnp.concatenate([lru_wa[l], lru_wx[l]], axis=-1).astype(BF16)
        ya = _mixer_a(mix3, conv_a_w[l], row(conv_a_b), w_gates, row(lru_ba), row(lru_bx),
                      row(lru_lam), mg[:, :d_lru], lru_heads)
        yb = _mixer_b(mix3, conv_b_w[l], mg[:, d_lru:d_lru + d_conv], 2 * lru_heads, conv_groups)
        yc = _attention(qkv3, mg[:, d_lru + d_conv:], sb_heads)

        pad = jnp.zeros((d, ROUTE_ROWS - n_experts - n_groups), F32)
        w_route_t = jnp.concatenate([w_router_expert[l], w_router_group[l], pad], axis=1).T.astype(BF16)
        b_route = jnp.concatenate([b_router_expert[l], b_router_group[l],
                                   jnp.zeros((ROUTE_ROWS - n_experts - n_groups,), F32)]).reshape(-1, 1)
        x1, h2, logits_t = _out_projection(ya.reshape(t, -1), yb.reshape(t, -1), yc.reshape(t, -1), xt,
                                           w_out[l].astype(BF16), row(norm_ffn), w_route_t, b_route)

        dest, gates, meta = _route(logits_t, n_experts, n_groups, row_block)
        src, bexp, nused = _tables(dest.reshape(-1), meta[:, 0], t, n_experts, n_blocks, row_block)
        y_tok = _experts(h2, w_gate[l].astype(BF16), w_up[l].astype(BF16), w_down[l].astype(BF16),
                         bexp, nused, src, row_block)
        xt = _combine(x1, y_tok, gates.T)
    return xt.reshape(bsz, s, d)
```

```python
import functools
import math

import jax
import jax.numpy as jnp
from jax import lax
from jax.experimental import pallas as pl
from jax.experimental.pallas import tpu as pltpu

EPS = 1e-6
LRU_C = 8.0
LOG2_E = 1.4426950408889634
HEAD_DIM = 128
SUBLANES = 8
EXPERTS_PER_GROUP = 8
ROUTE_ROWS = 48
VMEM_LIMIT = 56 * 1024 * 1024

F32 = jnp.float32
BF16 = jnp.bfloat16


def _tile(n, pref):
    if n <= pref:
        return n
    t = pref
    while n % t:
        t -= SUBLANES
    assert t > 0
    return t


def _params(sem):
    return pltpu.CompilerParams(dimension_semantics=sem, vmem_limit_bytes=VMEM_LIMIT)


def _rms(v):
    return v * lax.rsqrt(jnp.mean(v * v, axis=-1, keepdims=True) + EPS)


def _inproj_kernel(x_ref, g_ref, w_ref, qn_ref, kn_ref, mix_ref, qkv_ref, h_scr, *, n_mix, heads):
    j = pl.program_id(1)

    @pl.when(j == 0)
    def _():
        h_scr[...] = (_rms(x_ref[...]) * g_ref[...]).astype(BF16)

    acc = jnp.dot(h_scr[...], w_ref[...], preferred_element_type=F32)

    @pl.when(j < n_mix)
    def _():
        mix_ref[...] = acc

    def head_norm(gain_ref):
        for h in range(heads):
            sl = slice(h * HEAD_DIM, (h + 1) * HEAD_DIM)
            qkv_ref[:, sl] = (_rms(acc[:, sl]) * gain_ref[...]).astype(BF16)

    @pl.when(j == n_mix)
    def _():
        head_norm(qn_ref)

    @pl.when(j == n_mix + 1)
    def _():
        head_norm(kn_ref)

    @pl.when(j == n_mix + 2)
    def _():
        qkv_ref[...] = acc.astype(BF16)


def _in_projection(x, gain, w_bf16, layer, q_gain, k_gain, d_sb):
    t, d = x.shape
    d_in = w_bf16.shape[2]
    tn = d_sb
    assert d_in % tn == 0 and (d_in - 3 * d_sb) % tn == 0
    n_mix = (d_in - 3 * d_sb) // tn
    tm = _tile(t, 1024)
    kern = functools.partial(_inproj_kernel, n_mix=n_mix, heads=d_sb // HEAD_DIM)
    return pl.pallas_call(
        kern,
        grid=(t // tm, d_in // tn),
        in_specs=[
            pl.BlockSpec((tm, d), lambda i, j: (i, 0)),
            pl.BlockSpec((1, d), lambda i, j: (0, 0)),
            pl.BlockSpec((None, d, tn), lambda i, j: (layer, 0, j)),
            pl.BlockSpec((1, HEAD_DIM), lambda i, j: (0, 0)),
            pl.BlockSpec((1, HEAD_DIM), lambda i, j: (0, 0)),
        ],
        out_specs=[
            pl.BlockSpec((tm, tn), lambda i, j: (i, jnp.minimum(j, n_mix - 1))),
            pl.BlockSpec((tm, tn), lambda i, j: (i, jnp.maximum(j - n_mix, 0))),
        ],
        out_shape=[
            jax.ShapeDtypeStruct((t, n_mix * tn), F32),
            jax.ShapeDtypeStruct((t, 3 * d_sb), BF16),
        ],
        scratch_shapes=[pltpu.VMEM((tm, d), BF16)],
        compiler_params=_params(("arbitrary", "arbitrary")),
        name="in_projection",
    )(x, gain, w_bf16, q_gain, k_gain)


def _softplus(v):
    return jnp.maximum(v, 0.0) + jnp.log1p(jnp.exp(-jnp.abs(v)))


def _sigmoid(v):
    return 0.5 * (jnp.tanh(0.5 * v) + 1.0)


def _causal_conv(ext_scr, cols, new_rows, cw, ts):
    taps = cw.shape[0]
    ext_scr[SUBLANES:SUBLANES + ts, cols] = new_rows
    out = None
    for k in range(taps):
        lo = SUBLANES - (taps - 1) + k
        term = cw[k:k + 1, :] * ext_scr[lo:lo + ts, cols]
        out = term if out is None else out + term
    ext_scr[0:SUBLANES, cols] = ext_scr[ts:ts + SUBLANES, cols]
    return out


def _mixers_kernel(mix_ref, cwa_ref, cba_ref, wg_ref, ba_ref, bx_ref, lam_ref, cwb_ref, mn_ref,
                   out_ref, exta_scr, extb_scr, a_scr, u_scr, hc_scr, *, ts, heads, groups):
    si = pl.program_id(1)
    d_lru = heads * HEAD_DIM
    d_conv = groups * HEAD_DIM

    @pl.when(si == 0)
    def _():
        exta_scr[0:SUBLANES, :] = jnp.zeros((SUBLANES, d_lru), F32)
        extb_scr[0:SUBLANES, :] = jnp.zeros((SUBLANES, d_conv), F32)
        hc_scr[...] = jnp.zeros_like(hc_scr)

    sub = lax.broadcasted_iota(jnp.int32, (ts, HEAD_DIM), 0) & (SUBLANES - 1)

    for h in range(heads):
        cols = slice(h * HEAD_DIM, (h + 1) * HEAD_DIM)
        xc = _causal_conv(exta_scr, cols, mix_ref[0, :, cols], cwa_ref[:, cols], ts) + cba_ref[:, cols]
        gates = jnp.dot(xc.astype(BF16), wg_ref[h], preferred_element_type=F32)
        r = _sigmoid(gates[:, :HEAD_DIM] + ba_ref[:, cols])
        i = _sigmoid(gates[:, HEAD_DIM:] + bx_ref[:, cols])
        log_a = (-LRU_C) * r * _softplus(-lam_ref[:, cols])
        a = jnp.exp(log_a)
        u = jnp.sqrt(-jnp.tanh(log_a) * (a * a + 1.0)) * (i * xc)

        d = 1
        while d < SUBLANES:
            keep = sub >= d
            a_sh = jnp.where(keep, pltpu.roll(a, d, 0), 1.0)
            u_sh = jnp.where(keep, pltpu.roll(u, d, 0), 0.0)
            u = a * u_sh + u
            a = a * a_sh
            d *= 2
        a_scr[...] = a
        u_scr[...] = u

        def carry_step(c, hb):
            off = pl.multiple_of(c * SUBLANES, SUBLANES)
            hrow = u_scr[pl.ds(off, SUBLANES), :] + a_scr[pl.ds(off, SUBLANES), :] * hb
            u_scr[pl.ds(off, SUBLANES), :] = hrow
            return jnp.broadcast_to(hrow[SUBLANES - 1:SUBLANES, :], (SUBLANES, HEAD_DIM))

        hc_scr[:, cols] = lax.fori_loop(0, ts // SUBLANES, carry_step, hc_scr[:, cols], unroll=8)

        y = u_scr[...] * jax.nn.gelu(mix_ref[0, :, d_lru + h * HEAD_DIM:d_lru + (h + 1) * HEAD_DIM])
        out_ref[0, :, cols] = (_rms(y) * mn_ref[:, cols]).astype(BF16)

    for g in range(groups):
        cols = slice(g * HEAD_DIM, (g + 1) * HEAD_DIM)
        col = lambda part: slice(2 * d_lru + part * d_conv + g * HEAD_DIM,
                                 2 * d_lru + part * d_conv + (g + 1) * HEAD_DIM)
        conv = _causal_conv(extb_scr, cols, mix_ref[0, :, col(1)] * mix_ref[0, :, col(2)], cwb_ref[:, cols], ts)
        y = mix_ref[0, :, col(0)] * conv
        ocols = slice(d_lru + g * HEAD_DIM, d_lru + (g + 1) * HEAD_DIM)
        out_ref[0, :, ocols] = (_rms(y) * mn_ref[:, ocols]).astype(BF16)


def _mixers(mix3, conv_a_w, conv_a_b, w_gates, b_a, b_x, lam, conv_b_w, mix_gain):
    b, s, d_mix = mix3.shape
    heads = w_gates.shape[0]
    d_lru = heads * HEAD_DIM
    d_conv = conv_b_w.shape[1]
    groups = d_conv // HEAD_DIM
    assert d_mix == 2 * d_lru + 3 * d_conv
    ts = _tile(s, 512)
    assert max(conv_a_w.shape[0], conv_b_w.shape[0]) - 1 <= SUBLANES and ts % SUBLANES == 0
    kern = functools.partial(_mixers_kernel, ts=ts, heads=heads, groups=groups)
    full = lambda a: pl.BlockSpec(a.shape, lambda bi, si: (0,) * a.ndim)
    return pl.pallas_call(
        kern,
        grid=(b, s // ts),
        in_specs=[pl.BlockSpec((1, ts, d_mix), lambda bi, si: (bi, si, 0)),
                  full(conv_a_w), full(conv_a_b), full(w_gates), full(b_a), full(b_x), full(lam),
                  full(conv_b_w), full(mix_gain)],
        out_specs=pl.BlockSpec((1, ts, d_lru + d_conv), lambda bi, si: (bi, si, 0)),
        out_shape=jax.ShapeDtypeStruct((b, s, d_lru + d_conv), BF16),
        scratch_shapes=[
            pltpu.VMEM((ts + SUBLANES, d_lru), F32),
            pltpu.VMEM((ts + SUBLANES, d_conv), F32),
            pltpu.VMEM((ts, HEAD_DIM), F32),
            pltpu.VMEM((ts, HEAD_DIM), F32),
            pltpu.VMEM((SUBLANES, d_lru), F32),
        ],
        compiler_params=_params(("arbitrary", "arbitrary")),
        name="mixers_rglru_conv",
    )(mix3, conv_a_w, conv_a_b, w_gates, b_a, b_x, lam, conv_b_w, mix_gain)


def _attn_kernel(q_ref, k_ref, v_ref, m_ref, mn_ref, o_ref, acc_scr, run_scr, *, bq, bk, scale):
    qi = pl.program_id(2)
    q = q_ref[0]
    n_sub = bq // bk

    def chunk(c0, run, diagonal):
        kc = k_ref[0, pl.ds(c0, bq), :]
        vc = v_ref[0, pl.ds(c0, bq), :]
        s = lax.dot_general(q, kc, (((1,), (1,)), ((), ())), preferred_element_type=F32)
        z = s * scale
        ls = jnp.minimum(z, 0.0) - jnp.log(1.0 + jnp.exp2(jnp.abs(s) * (-scale * LOG2_E)))
        lk = ls - z
        if diagonal:
            mask = (lax.broadcasted_iota(jnp.int32, (bq, bq), 1)
                    < lax.broadcasted_iota(jnp.int32, (bq, bq), 0))
            lk = jnp.where(mask, lk, 0.0)
        ws = [None] * n_sub
        for k in reversed(range(n_sub)):
            sl = slice(k * bk, (k + 1) * bk)
            hi = lk[:, sl].astype(BF16)
            lo = (lk[:, sl] - hi.astype(F32)).astype(BF16)
            sums = jnp.dot(jnp.concatenate([hi, lo], axis=1), m_ref[...], preferred_element_type=F32)
            w = jnp.exp(ls[:, sl] + sums[:, :bk] + run)
            if diagonal:
                w = jnp.where(mask[:, sl], w, 0.0)
            ws[k] = w.astype(BF16)
            run = run + sums[:, bk:]
        pv = jnp.dot(jnp.concatenate(ws, axis=1), vc, preferred_element_type=F32)
        return pv, run

    acc, run = chunk(pl.multiple_of(qi * bq, bq), jnp.zeros((bq, HEAD_DIM), F32), True)
    acc_scr[...] = acc
    run_scr[...] = run

    def chunk_pair(p, _):
        pv_a, run_a = chunk(pl.multiple_of((qi - 1 - 2 * p) * bq, bq), run_scr[...], False)
        pv_b, run_b = chunk(pl.multiple_of((qi - 2 - 2 * p) * bq, bq), run_a, False)
        acc_scr[...] += pv_a + pv_b
        run_scr[...] = run_b
        return 0

    lax.fori_loop(0, qi // 2, chunk_pair, 0)

    @pl.when(qi % 2 == 1)
    def _():
        pv, _ = chunk(0, run_scr[...], False)
        acc_scr[...] += pv

    o_ref[0] = (_rms(acc_scr[...]) * mn_ref[...]).astype(BF16)


def _suffix_matrix(bk):
    j = jnp.arange(bk)[:, None]
    s = jnp.arange(bk)[None, :]
    half = jnp.concatenate([(j > s).astype(BF16), jnp.ones((bk, HEAD_DIM), BF16)], axis=1)
    return jnp.concatenate([half, half], axis=0)


def _attention(qkv3, mix_gain, heads):
    b, s, _ = qkv3.shape
    bk = HEAD_DIM
    bq = _tile(s, 512)
    assert bq % bk == 0 and s % bq == 0
    kern = functools.partial(_attn_kernel, bq=bq, bk=bk, scale=1.0 / math.sqrt(HEAD_DIM))
    return pl.pallas_call(
        kern,
        grid=(b, heads, s // bq),
        in_specs=[
            pl.BlockSpec((1, bq, HEAD_DIM), lambda bi, h, qi: (bi, qi, h)),
            pl.BlockSpec((1, s, HEAD_DIM), lambda bi, h, qi: (bi, 0, heads + h)),
            pl.BlockSpec((1, s, HEAD_DIM), lambda bi, h, qi: (bi, 0, 2 * heads + h)),
            pl.BlockSpec((2 * bk, bk + HEAD_DIM), lambda bi, h, qi: (0, 0)),
            pl.BlockSpec((1, HEAD_DIM), lambda bi, h, qi: (0, h)),
        ],
        out_specs=pl.BlockSpec((1, bq, HEAD_DIM), lambda bi, h, qi: (bi, qi, h)),
        out_shape=jax.ShapeDtypeStruct((b, s, heads * HEAD_DIM), BF16),
        scratch_shapes=[pltpu.VMEM((bq, HEAD_DIM), F32), pltpu.VMEM((bq, HEAD_DIM), F32)],
        compiler_params=_params(("arbitrary", "arbitrary", "arbitrary")),
        name="stickbreak_attention",
    )(qkv3, qkv3, qkv3, _suffix_matrix(bk), mix_gain)


def _outproj_kernel(yab_ref, yc_ref, x_ref, w_ref, g_ref, wr_ref, br_ref,
                    x1_ref, h2_ref, lg_ref, *, dab):
    acc = jnp.dot(yab_ref[...], w_ref[0:dab, :], preferred_element_type=F32)
    acc = acc + jnp.dot(yc_ref[...], w_ref[dab:, :], preferred_element_type=F32)
    x1 = x_ref[...] + acc
    x1_ref[...] = x1
    h2 = _rms(x1) * g_ref[...]
    _store_token_major(h2_ref, 0, h2)
    lg = lax.dot_general(wr_ref[...], h2.astype(BF16), (((1,), (1,)), ((), ())),
                         preferred_element_type=F32)
    lg_ref[...] = lg + br_ref[...]


def _store_token_major(ref, first_row, val):
    rows, d = val.shape
    rt = d // HEAD_DIM
    for j in range(rt):
        ref[pl.ds(first_row * rt + j, rows, stride=rt), :] = val[:, j * HEAD_DIM:(j + 1) * HEAD_DIM]


def _load_token_major(ref, first_row, rows, rt):
    return [ref[pl.ds(first_row * rt + j, rows, stride=rt), :] for j in range(rt)]


def _out_projection(yab, yc, x, w_bf16, layer, gain, w_route_t, b_route):
    t, d = x.shape
    dab, dc = yab.shape[1], yc.shape[1]
    rt = d // HEAD_DIM
    tm = _tile(t, 256)
    kern = functools.partial(_outproj_kernel, dab=dab)
    row = lambda n: pl.BlockSpec((tm, n), lambda i: (i, 0))
    full = lambda shp: pl.BlockSpec(shp, lambda i: (0, 0))
    return pl.pallas_call(
        kern,
        grid=(t // tm,),
        in_specs=[row(dab), row(dc), row(d),
                  pl.BlockSpec((None, dab + dc, d), lambda i: (layer, 0, 0)), full((1, d)),
                  full((ROUTE_ROWS, d)), full((ROUTE_ROWS, 1))],
        out_specs=[row(d), pl.BlockSpec((tm * rt, HEAD_DIM), lambda i: (i, 0)),
                   pl.BlockSpec((ROUTE_ROWS, tm), lambda i: (0, i))],
        out_shape=[jax.ShapeDtypeStruct((t, d), F32), jax.ShapeDtypeStruct((t * rt, HEAD_DIM), F32),
                   jax.ShapeDtypeStruct((ROUTE_ROWS, t), F32)],
        compiler_params=_params(("arbitrary",)),
        name="out_projection",
    )(yab, yc, x, w_bf16, gain, w_route_t, b_route)


def _route_kernel(lg_ref, tri_ref, ones_ref, ltri_ref, dest_ref, gate_ref, meta_ref,
                  cnt_scr, pst_scr, run_scr, *, n_experts, n_groups, row_block, tl):
    ph = pl.program_id(0)
    i = pl.program_id(1)
    epg = n_experts // n_groups

    @pl.when((ph == 0) & (i == 0))
    def _():
        cnt_scr[...] = jnp.zeros_like(cnt_scr)

    lg = lg_ref[...]
    grp = lg[n_experts:n_experts + n_groups, :]
    gmax = jnp.max(grp, axis=0, keepdims=True)
    giota = lax.broadcasted_iota(jnp.int32, grp.shape, 0)
    gidx = jnp.min(jnp.where(grp == gmax, giota, n_groups), axis=0, keepdims=True)
    p_grp = 1.0 / jnp.sum(jnp.exp(grp - gmax), axis=0, keepdims=True)

    sel = lg[0:epg, :]
    for g in range(1, n_groups):
        sel = jnp.where(gidx == g, lg[g * epg:(g + 1) * epg, :], sel)
    eiota = lax.broadcasted_iota(jnp.int32, sel.shape, 0)
    m1 = jnp.max(sel, axis=0, keepdims=True)
    i1 = jnp.min(jnp.where(sel == m1, eiota, epg), axis=0, keepdims=True)
    rest = jnp.where(eiota == i1, -jnp.inf, sel)
    m2 = jnp.max(rest, axis=0, keepdims=True)
    i2 = jnp.min(jnp.where(rest == m2, eiota, epg), axis=0, keepdims=True)
    e = jnp.exp(m2 - m1)
    den = 1.0 + e
    ex1 = gidx * epg + i1
    ex2 = gidx * epg + i2

    xiota = lax.broadcasted_iota(jnp.int32, (n_experts, tl), 0)
    hit1 = xiota == ex1
    hit2 = xiota == ex2
    onehot = (hit1 | hit2).astype(BF16)
    tile_cnt = jnp.dot(onehot, ones_ref[...], preferred_element_type=F32)

    @pl.when(ph == 0)
    def _():
        cnt_scr[...] += tile_cnt

    @pl.when((ph == 1) & (i == 0))
    def _():
        cnt = cnt_scr[...].astype(jnp.int32)
        nblk = (cnt + (row_block - 1)) // row_block
        pst = jnp.dot(ltri_ref[...], nblk.astype(F32).astype(BF16), preferred_element_type=F32)
        pst_scr[...] = pst
        run_scr[...] = jnp.zeros_like(run_scr)
        meta_ref[0:n_experts, :] = pst.astype(jnp.int32)
        meta_ref[n_experts:2 * n_experts, :] = cnt

    @pl.when(ph == 1)
    def _():
        rank = jnp.dot(onehot, tri_ref[...], preferred_element_type=F32)
        base = pst_scr[...] * float(row_block) + run_scr[...]
        slot = jnp.concatenate([base] * (tl // HEAD_DIM), axis=1) + rank
        d1 = jnp.sum(jnp.where(hit1, slot, 0.0), axis=0, keepdims=True)
        d2 = jnp.sum(jnp.where(hit2, slot, 0.0), axis=0, keepdims=True)
        dest_ref[0:1, :] = d1.astype(jnp.int32)
        dest_ref[1:2, :] = d2.astype(jnp.int32)
        gate_ref[0:1, :] = p_grp * (1.0 / den)
        gate_ref[1:2, :] = p_grp * (e / den)
        run_scr[...] += tile_cnt


def _route(logits_t, n_experts, n_groups, row_block):
    t = logits_t.shape[1]
    tl = _tile(t, 512)
    assert tl % HEAD_DIM == 0
    a = jnp.arange(tl)
    tri = (a[:, None] < a[None, :]).astype(BF16)
    ones = jnp.ones((tl, HEAD_DIM), BF16)
    ea = jnp.arange(n_experts)
    ltri = (ea[None, :] < ea[:, None]).astype(BF16)
    kern = functools.partial(_route_kernel, n_experts=n_experts, n_groups=n_groups,
                             row_block=row_block, tl=tl)
    full = lambda shp: pl.BlockSpec(shp, lambda p, i: (0, 0))
    return pl.pallas_call(
        kern,
        grid=(2, t // tl),
        in_specs=[pl.BlockSpec((ROUTE_ROWS, tl), lambda p, i: (0, i)),
                  full((tl, tl)), full((tl, HEAD_DIM)), full((n_experts, n_experts))],
        out_specs=[pl.BlockSpec((2, tl), lambda p, i: (0, i * p)),
                   pl.BlockSpec((2, tl), lambda p, i: (0, i * p)),
                   full((2 * n_experts, HEAD_DIM))],
        out_shape=[jax.ShapeDtypeStruct((2, t), jnp.int32), jax.ShapeDtypeStruct((2, t), F32),
                   jax.ShapeDtypeStruct((2 * n_experts, HEAD_DIM), jnp.int32)],
        scratch_shapes=[pltpu.VMEM((n_experts, HEAD_DIM), F32),
                        pltpu.VMEM((n_experts, HEAD_DIM), F32),
                        pltpu.VMEM((n_experts, HEAD_DIM), F32)],
        compiler_params=_params(("arbitrary", "arbitrary")),
        name="moe_route",
    )(logits_t, tri, ones, ltri)


PAD_ROW = -1


def _tables_kernel(dest_ref, meta_ref, src_ref, bexp_ref, nused_ref, *,
                   t, n_experts, n_blocks, row_block):
    def mark_pad(ri, _):
        src_ref[ri] = PAD_ROW
        return 0

    lax.fori_loop(0, n_blocks * row_block, mark_pad, 0, unroll=16)

    def fill(tok, _):
        src_ref[dest_ref[tok]] = tok
        src_ref[dest_ref[t + tok]] = t + tok
        return 0

    lax.fori_loop(0, t, fill, 0, unroll=8)

    def clear(bi, _):
        bexp_ref[bi] = n_experts - 1
        return 0

    lax.fori_loop(0, n_blocks, clear, 0)

    def per_expert(ex, used):
        first = meta_ref[ex]
        nblk = (meta_ref[n_experts + ex] + (row_block - 1)) // row_block

        def per_block(bj, _):
            bexp_ref[first + bj] = ex
            return 0

        lax.fori_loop(0, nblk, per_block, 0)
        return first + nblk

    nused_ref[0] = lax.fori_loop(0, n_experts, per_expert, 0)


def _tables(dest_flat, meta_flat, t, n_experts, n_blocks, row_block):
    kern = functools.partial(_tables_kernel, t=t, n_experts=n_experts, n_blocks=n_blocks,
                             row_block=row_block)
    smem = pl.BlockSpec(memory_space=pltpu.SMEM)
    return pl.pallas_call(
        kern,
        in_specs=[smem, smem],
        out_specs=[smem, smem, smem],
        out_shape=[jax.ShapeDtypeStruct((n_blocks * row_block,), jnp.int32),
                   jax.ShapeDtypeStruct((n_blocks,), jnp.int32),
                   jax.ShapeDtypeStruct((1,), jnp.int32)],
        name="moe_tables",
    )(dest_flat, meta_flat)


def _expert_kernel(bexp_ref, nused_ref, src_ref, h2_hbm, wg_ref, wu_ref, wd_ref,
                   y_hbm, xbuf, ybuf, wg_bf, wu_bf, wd_bf, gsem, ssem, *, t, rt, row_block, n_blocks):
    i = pl.program_id(0)
    n_used = nused_ref[0]
    slot = i % 2
    buf_rows = row_block * rt

    def buf(ref, s):
        return ref.at[pl.ds(pl.multiple_of(s * buf_rows, buf_rows), buf_rows), :]

    def start_gather(blk, s):
        def body(r, _):
            src = src_ref[blk * row_block + r]
            if t & (t - 1) == 0:
                tok = src & (t - 1)
            else:
                tok = jnp.where(src < 0, 0, jnp.where(src >= t, src - t, src))
            pltpu.make_async_copy(h2_hbm.at[pl.ds(pl.multiple_of(tok * rt, rt), rt), :],
                                  xbuf.at[pl.ds(pl.multiple_of((s * row_block + r) * rt, rt), rt), :],
                                  gsem.at[s]).start()
            return 0
        lax.fori_loop(0, row_block, body, 0, unroll=8)

    def start_scatter(blk, s):
        def body(r, _):
            src = src_ref[blk * row_block + r]
            dst = jnp.where(src < 0, 2 * t + s * row_block + r, src)
            pltpu.make_async_copy(ybuf.at[pl.ds(pl.multiple_of((s * row_block + r) * rt, rt), rt), :],
                                  y_hbm.at[pl.ds(pl.multiple_of(dst * rt, rt), rt), :],
                                  ssem.at[s]).start()
            return 0
        lax.fori_loop(0, row_block, body, 0, unroll=8)

    def wait_gather(s):
        pltpu.make_async_copy(buf(xbuf, s), buf(xbuf, s), gsem.at[s]).wait()

    def wait_scatter(s):
        pltpu.make_async_copy(buf(ybuf, s), buf(ybuf, s), ssem.at[s]).wait()

    @pl.when(i == 0)
    def _():
        ybuf[...] = jnp.zeros_like(ybuf)
        for s in range(2):
            spare = y_hbm.at[pl.ds((2 * t + s * row_block) * rt, buf_rows), :]
            pltpu.make_async_copy(buf(ybuf, s), spare, ssem.at[s]).start()
        for s in range(2):
            wait_scatter(s)
        start_gather(0, 0)

    @pl.when((i == 0) | (bexp_ref[i] != bexp_ref[jnp.maximum(i - 1, 0)]))
    def _():
        wg_bf[...] = wg_ref[...].astype(BF16)
        wu_bf[...] = wu_ref[...].astype(BF16)
        wd_bf[...] = wd_ref[...].astype(BF16)

    @pl.when(i + 1 < n_used)
    def _():
        start_gather(i + 1, 1 - slot)

    @pl.when(i < n_used)
    def _():
        wait_gather(slot)

        @pl.when(i >= 2)
        def _():
            wait_scatter(slot)

        x = jnp.concatenate(_load_token_major(xbuf, slot * row_block, row_block, rt), axis=1).astype(BF16)
        gt = jnp.dot(x, wg_bf[...], preferred_element_type=F32)
        up = jnp.dot(x, wu_bf[...], preferred_element_type=F32)
        hid = (jax.nn.silu(gt) * up).astype(BF16)
        _store_token_major(ybuf, slot * row_block, jnp.dot(hid, wd_bf[...], preferred_element_type=F32))
        start_scatter(i, slot)

    @pl.when(i == n_blocks - 1)
    def _():
        wait_scatter((n_used - 1) % 2)

        @pl.when(n_used >= 2)
        def _():
            wait_scatter(n_used % 2)


def _experts(h2_tm, t, w_gate, w_up, w_down, layer, bexp, nused, src, row_block):
    rt = h2_tm.shape[0] // t
    d = rt * HEAD_DIM
    n_blocks = bexp.shape[0]
    de = w_gate.shape[3]
    kern = functools.partial(_expert_kernel, t=t, rt=rt, row_block=row_block, n_blocks=n_blocks)
    grid_spec = pltpu.PrefetchScalarGridSpec(
        num_scalar_prefetch=3,
        grid=(n_blocks,),
        in_specs=[
            pl.BlockSpec(memory_space=pl.ANY),
            pl.BlockSpec((None, None, d, de), lambda i, be, nu, sr: (layer, be[i], 0, 0)),
            pl.BlockSpec((None, None, d, de), lambda i, be, nu, sr: (layer, be[i], 0, 0)),
            pl.BlockSpec((None, None, de, d), lambda i, be, nu, sr: (layer, be[i], 0, 0)),
        ],
        out_specs=pl.BlockSpec(memory_space=pl.ANY),
        scratch_shapes=[
            pltpu.VMEM((2 * row_block * rt, HEAD_DIM), F32),
            pltpu.VMEM((2 * row_block * rt, HEAD_DIM), F32),
            pltpu.VMEM((d, de), BF16),
            pltpu.VMEM((d, de), BF16),
            pltpu.VMEM((de, d), BF16),
            pltpu.SemaphoreType.DMA((2,)),
            pltpu.SemaphoreType.DMA((2,)),
        ],
    )
    return pl.pallas_call(
        kern,
        grid_spec=grid_spec,
        out_shape=jax.ShapeDtypeStruct(((2 * t + 2 * row_block) * rt, HEAD_DIM), F32),
        compiler_params=_params(("arbitrary",)),
        name="moe_experts",
    )(bexp, nused, src, h2_tm, w_gate, w_up, w_down)


def _combine_kernel(x_ref, y0_ref, y1_ref, g_ref, o_ref, *, tm, rt):
    g = g_ref[...]
    g0 = jnp.broadcast_to(g[:, 0:1], (tm, HEAD_DIM))
    g1 = jnp.broadcast_to(g[:, 1:2], (tm, HEAD_DIM))
    for j in range(rt):
        sl = slice(j * HEAD_DIM, (j + 1) * HEAD_DIM)
        y0 = y0_ref[pl.ds(j, tm, stride=rt), :]
        y1 = y1_ref[pl.ds(j, tm, stride=rt), :]
        o_ref[:, sl] = x_ref[:, sl] + (y0 * g0 + y1 * g1)


def _combine(x1, y_tok, gates_t):
    t, d = x1.shape
    rt = d // HEAD_DIM
    tm = _tile(t, 512)
    nb = t // tm
    return pl.pallas_call(
        functools.partial(_combine_kernel, tm=tm, rt=rt),
        grid=(nb,),
        in_specs=[pl.BlockSpec((tm, d), lambda i: (i, 0)),
                  pl.BlockSpec((tm * rt, HEAD_DIM), lambda i: (i, 0)),
                  pl.BlockSpec((tm * rt, HEAD_DIM), lambda i: (i + nb, 0)),
                  pl.BlockSpec((tm, 2), lambda i: (i, 0))],
        out_specs=pl.BlockSpec((tm, d), lambda i: (i, 0)),
        out_shape=jax.ShapeDtypeStruct((t, d), F32),
        compiler_params=_params(("arbitrary",)),
        name="moe_combine",
    )(x1, y_tok, y_tok, gates_t)


MOE_ROW_BLOCK = 256


def kernel(x, norm_mix, w_in, conv_a_w, conv_a_b, lru_wa, lru_ba, lru_wx, lru_bx, lru_lam,
           conv_b_w, q_norm, k_norm, mix_norm, w_out, norm_ffn, w_router_group, b_router_group,
           w_router_expert, b_router_expert, w_gate, w_up, w_down):
    bsz, s, d = x.shape
    t = bsz * s
    depth = w_in.shape[0]
    lru_heads = lru_wa.shape[1]
    d_lru = lru_heads * HEAD_DIM
    d_conv = conv_b_w.shape[-1]
    d_sb =(w_in.shape[-1] - 2 * d_lru - 3 * d_conv) // 3
    sb_heads = d_sb // HEAD_DIM
    n_groups = w_router_group.shape[-1]
    n_experts = w_router_expert.shape[-1]
    assert n_experts // n_groups == EXPERTS_PER_GROUP and n_experts + n_groups <= ROUTE_ROWS
    row_block = MOE_ROW_BLOCK
    n_blocks = -(-(2 * t) // row_block) + n_experts

    xt = x.reshape(t, d)
    w_in_bf = w_in.astype(BF16)
    w_out_bf = w_out.astype(BF16)
    for l in range(depth):
        row = lambda v: v[l].reshape(1, -1)
        mix, qkv = _in_projection(xt, row(norm_mix), w_in_bf, l, row(q_norm), row(k_norm), d_sb)
        mix3 = mix.reshape(bsz, s, -1)
        qkv3 = qkv.reshape(bsz, s, -1)
        mg = row(mix_norm)
        w_gates = jnp.concatenate([lru_wa[l], lru_wx[l]], axis=-1).astype(BF16)
        yab = _mixers(mix3, conv_a_w[l], row(conv_a_b), w_gates, row(lru_ba), row(lru_bx),
                      row(lru_lam), conv_b_w[l], mg[:, :d_lru + d_conv])
        yc = _attention(qkv3, mg[:, d_lru + d_conv:], sb_heads)

        pad = jnp.zeros((d, ROUTE_ROWS - n_experts - n_groups), F32)
        w_route_t = jnp.concatenate([w_router_expert[l], w_router_group[l], pad], axis=1).T.astype(BF16)
        b_route = jnp.concatenate([b_router_expert[l], b_router_group[l],
                                   jnp.zeros((ROUTE_ROWS - n_experts - n_groups,), F32)]).reshape(-1, 1)
        x1, h2, logits_t = _out_projection(yab.reshape(t, -1), yc.reshape(t, -1), xt,
                                           w_out_bf, l, row(norm_ffn), w_route_t, b_route)

        dest, gates, meta = _route(logits_t, n_experts, n_groups, row_block)
        src, bexp, nused = _tables(dest.reshape(-1), meta[:, 0], t, n_experts, n_blocks, row_block)
        y_tok = _experts(h2, t, w_gate, w_up, w_down, l, bexp, nused, src, row_block)
        xt = _combine(x1, y_tok, gates.T)
    return xt.reshape(bsz, s, d)
```

```python
import functools
import math

import jax
import jax.numpy as jnp
from jax import lax
from jax.experimental import pallas as pl
from jax.experimental.pallas import tpu as pltpu

EPS = 1e-6
LRU_C = 8.0
LOG2_E = 1.4426950408889634
HEAD_DIM = 128
SUBLANES = 8
EXPERTS_PER_GROUP = 8
ROUTE_ROWS = 48
VMEM_LIMIT = 56 * 1024 * 1024

F32 = jnp.float32
BF16 = jnp.bfloat16


def _tile(n, pref):
    if n <= pref:
        return n
    t = pref
    while n % t:
        t -= SUBLANES
    assert t > 0
    return t


def _params(sem):
    return pltpu.CompilerParams(dimension_semantics=sem, vmem_limit_bytes=VMEM_LIMIT)


def _rms(v):
    return v * lax.rsqrt(jnp.mean(v * v, axis=-1, keepdims=True) + EPS)


def _inproj_kernel(x_ref, g_ref, w_ref, qn_ref, kn_ref, mix_ref, qkv_ref, h_scr, *, n_mix, heads):
    j = pl.program_id(1)

    @pl.when(j == 0)
    def _():
        h_scr[...] = (_rms(x_ref[...]) * g_ref[...]).astype(BF16)

    acc = jnp.dot(h_scr[...], w_ref[...], preferred_element_type=F32)

    @pl.when(j < n_mix)
    def _():
        mix_ref[...] = acc

    def head_norm(gain_ref):
        for h in range(heads):
            sl = slice(h * HEAD_DIM, (h + 1) * HEAD_DIM)
            qkv_ref[:, sl] = (_rms(acc[:, sl]) * gain_ref[...]).astype(BF16)

    @pl.when(j == n_mix)
    def _():
        head_norm(qn_ref)

    @pl.when(j == n_mix + 1)
    def _():
        head_norm(kn_ref)

    @pl.when(j == n_mix + 2)
    def _():
        qkv_ref[...] = acc.astype(BF16)


def _in_projection(x, gain, w_bf16, layer, q_gain, k_gain, d_sb):
    t, d = x.shape
    d_in = w_bf16.shape[2]
    tn = d_sb
    assert d_in % tn == 0 and (d_in - 3 * d_sb) % tn == 0
    n_mix = (d_in - 3 * d_sb) // tn
    tm = _tile(t, 1024)
    kern = functools.partial(_inproj_kernel, n_mix=n_mix, heads=d_sb // HEAD_DIM)
    return pl.pallas_call(
        kern,
        grid=(t // tm, d_in // tn),
        in_specs=[
            pl.BlockSpec((tm, d), lambda i, j: (i, 0)),
            pl.BlockSpec((1, d), lambda i, j: (0, 0)),
            pl.BlockSpec((None, d, tn), lambda i, j: (layer, 0, j)),
            pl.BlockSpec((1, HEAD_DIM), lambda i, j: (0, 0)),
            pl.BlockSpec((1, HEAD_DIM), lambda i, j: (0, 0)),
        ],
        out_specs=[
            pl.BlockSpec((tm, tn), lambda i, j: (i, jnp.minimum(j, n_mix - 1))),
            pl.BlockSpec((tm, tn), lambda i, j: (i, jnp.maximum(j - n_mix, 0))),
        ],
        out_shape=[
            jax.ShapeDtypeStruct((t, n_mix * tn), F32),
            jax.ShapeDtypeStruct((t, 3 * d_sb), BF16),
        ],
        scratch_shapes=[pltpu.VMEM((tm, d), BF16)],
        compiler_params=_params(("arbitrary", "arbitrary")),
        name="in_projection",
    )(x, gain, w_bf16, q_gain, k_gain)


def _softplus(v):
    return jnp.maximum(v, 0.0) + jnp.log1p(jnp.exp(-jnp.abs(v)))


def _sigmoid(v):
    return 0.5 * (jnp.tanh(0.5 * v) + 1.0)


def _causal_conv(ext_scr, cols, new_rows, cw, ts):
    taps = cw.shape[0]
    ext_scr[SUBLANES:SUBLANES + ts, cols] = new_rows
    out = None
    for k in range(taps):
        lo = SUBLANES - (taps - 1) + k
        term = cw[k:k + 1, :] * ext_scr[lo:lo + ts, cols]
        out = term if out is None else out + term
    ext_scr[0:SUBLANES, cols] = ext_scr[ts:ts + SUBLANES, cols]
    return out


def _mixers_kernel(mix_ref, cwa_ref, cba_ref, wg_ref, ba_ref, bx_ref, lam_ref, cwb_ref, mn_ref,
                   out_ref, exta_scr, extb_scr, a_scr, u_scr, hc_scr, *, ts, heads, groups):
    si = pl.program_id(1)
    d_lru = heads * HEAD_DIM
    d_conv = groups * HEAD_DIM

    @pl.when(si == 0)
    def _():
        exta_scr[0:SUBLANES, :] = jnp.zeros((SUBLANES, d_lru), F32)
        extb_scr[0:SUBLANES, :] = jnp.zeros((SUBLANES, d_conv), F32)
        hc_scr[...] = jnp.zeros_like(hc_scr)

    sub = lax.broadcasted_iota(jnp.int32, (ts, HEAD_DIM), 0) & (SUBLANES - 1)

    for h in range(heads):
        cols = slice(h * HEAD_DIM, (h + 1) * HEAD_DIM)
        xc = _causal_conv(exta_scr, cols, mix_ref[0, :, cols], cwa_ref[:, cols], ts) + cba_ref[:, cols]
        gates = jnp.dot(xc.astype(BF16), wg_ref[h], preferred_element_type=F32)
        r = _sigmoid(gates[:, :HEAD_DIM] + ba_ref[:, cols])
        i = _sigmoid(gates[:, HEAD_DIM:] + bx_ref[:, cols])
        log_a = (-LRU_C) * r * _softplus(-lam_ref[:, cols])
        a = jnp.exp(log_a)
        u = jnp.sqrt(-jnp.tanh(log_a) * (a * a + 1.0)) * (i * xc)

        d = 1
        while d < SUBLANES:
            keep = sub >= d
            a_sh = jnp.where(keep, pltpu.roll(a, d, 0), 1.0)
            u_sh = jnp.where(keep, pltpu.roll(u, d, 0), 0.0)
            u = a * u_sh + u
            a = a * a_sh
            d *= 2
        a_scr[...] = a
        u_scr[...] = u

        def carry_step(c, hb):
            off = pl.multiple_of(c * SUBLANES, SUBLANES)
            hrow = u_scr[pl.ds(off, SUBLANES), :] + a_scr[pl.ds(off, SUBLANES), :] * hb
            u_scr[pl.ds(off, SUBLANES), :] = hrow
            return jnp.broadcast_to(hrow[SUBLANES - 1:SUBLANES, :], (SUBLANES, HEAD_DIM))

        hc_scr[:, cols] = lax.fori_loop(0, ts // SUBLANES, carry_step, hc_scr[:, cols], unroll=8)

        y = u_scr[...] * jax.nn.gelu(mix_ref[0, :, d_lru + h * HEAD_DIM:d_lru + (h + 1) * HEAD_DIM])
        out_ref[0, :, cols] = (_rms(y) * mn_ref[:, cols]).astype(BF16)

    for g in range(groups):
        cols = slice(g * HEAD_DIM, (g + 1) * HEAD_DIM)
        col = lambda part: slice(2 * d_lru + part * d_conv + g * HEAD_DIM,
                                 2 * d_lru + part * d_conv + (g + 1) * HEAD_DIM)
        conv = _causal_conv(extb_scr, cols, mix_ref[0, :, col(1)] * mix_ref[0, :, col(2)], cwb_ref[:, cols], ts)
        y = mix_ref[0, :, col(0)] * conv
        ocols = slice(d_lru + g * HEAD_DIM, d_lru + (g + 1) * HEAD_DIM)
        out_ref[0, :, ocols] = (_rms(y) * mn_ref[:, ocols]).astype(BF16)


def _mixers(mix3, conv_a_w, conv_a_b, w_gates, b_a, b_x, lam, conv_b_w, mix_gain):
    b, s, d_mix = mix3.shape
    heads = w_gates.shape[0]
    d_lru = heads * HEAD_DIM
    d_conv = conv_b_w.shape[1]
    groups = d_conv // HEAD_DIM
    assert d_mix == 2 * d_lru + 3 * d_conv
    ts = _tile(s, 512)
    assert max(conv_a_w.shape[0], conv_b_w.shape[0]) - 1 <= SUBLANES and ts % SUBLANES == 0
    kern = functools.partial(_mixers_kernel, ts=ts, heads=heads, groups=groups)
    full = lambda a: pl.BlockSpec(a.shape, lambda bi, si: (0,) * a.ndim)
    return pl.pallas_call(
        kern,
        grid=(b, s // ts),
        in_specs=[pl.BlockSpec((1, ts, d_mix), lambda bi, si: (bi, si, 0)),
                  full(conv_a_w), full(conv_a_b), full(w_gates), full(b_a), full(b_x), full(lam),
                  full(conv_b_w), full(mix_gain)],
        out_specs=pl.BlockSpec((1, ts, d_lru + d_conv), lambda bi, si: (bi, si, 0)),
        out_shape=jax.ShapeDtypeStruct((b, s, d_lru + d_conv), BF16),
        scratch_shapes=[
            pltpu.VMEM((ts + SUBLANES, d_lru), F32),
            pltpu.VMEM((ts + SUBLANES, d_conv), F32),
            pltpu.VMEM((ts, HEAD_DIM), F32),
            pltpu.VMEM((ts, HEAD_DIM), F32),
            pltpu.VMEM((SUBLANES, d_lru), F32),
        ],
        compiler_params=_params(("arbitrary", "arbitrary")),
        name="mixers_rglru_conv",
    )(mix3, conv_a_w, conv_a_b, w_gates, b_a, b_x, lam, conv_b_w, mix_gain)


def _attn_kernel(q_ref, k_ref, v_ref, m_ref, mn_ref, o_ref, acc_scr, run_scr, *, bq, bk, scale):
    qi = pl.program_id(2)
    q = q_ref[0]
    n_sub = bq // bk

    def log_sigmoids(s):
        z = s * scale
        ls = jnp.minimum(z, 0.0) - jnp.log(1.0 + jnp.exp2(jnp.abs(s) * (-scale * LOG2_E)))
        return ls, ls - z

    def suffix_sums(lk):
        hi = lk.astype(BF16)
        lo = (lk - hi.astype(F32)).astype(BF16)
        return jnp.dot(jnp.concatenate([hi, lo], axis=1), m_ref[...], preferred_element_type=F32)

    def chunk(c0, run):
        kc = k_ref[0, pl.ds(c0, bq), :]
        vc = v_ref[0, pl.ds(c0, bq), :]
        ls, lk = log_sigmoids(lax.dot_general(q, kc, (((1,), (1,)), ((), ())), preferred_element_type=F32))
        ws = [None] * n_sub
        for k in reversed(range(n_sub)):
            sl = slice(k * bk, (k + 1) * bk)
            sums = suffix_sums(lk[:, sl])
            ws[k] = jnp.exp(ls[:, sl] + sums[:, :bk] + run).astype(BF16)
            run = run + sums[:, bk:]
        pv = jnp.dot(jnp.concatenate(ws, axis=1), vc, preferred_element_type=F32)
        return pv, run

    def diagonal_chunk(c0):
        kc = k_ref[0, pl.ds(c0, bq), :]
        vc = v_ref[0, pl.ds(c0, bq), :]
        s = lax.dot_general(q, kc, (((1,), (1,)), ((), ())), preferred_element_type=F32)
        run = jnp.zeros((bq, HEAD_DIM), F32)
        ws = [None] * n_sub
        for k in reversed(range(n_sub)):
            r0 = k * bk
            rows = bq - r0
            ls, lk = log_sigmoids(s[r0:, r0:r0 + bk])
            mask = (lax.broadcasted_iota(jnp.int32, (rows, bk), 1)
                    < lax.broadcasted_iota(jnp.int32, (rows, bk), 0))
            sums = suffix_sums(jnp.where(mask, lk, 0.0))
            w = jnp.where(mask, jnp.exp(ls + sums[:, :bk] + run[r0:]), 0.0).astype(BF16)
            seen = run[r0:] + sums[:, bk:]
            if r0:
                w = jnp.concatenate([jnp.zeros((r0, bk), BF16), w], axis=0)
                seen = jnp.concatenate([run[:r0], seen], axis=0)
            ws[k] = w
            run = seen
        pv = jnp.dot(jnp.concatenate(ws, axis=1), vc, preferred_element_type=F32)
        return pv, run

    acc, run = diagonal_chunk(pl.multiple_of(qi * bq, bq))
    acc_scr[...] = acc
    run_scr[...] = run

    def chunk_pair(p, _):
        pv_a, run_a = chunk(pl.multiple_of((qi - 1 - 2 * p) * bq, bq), run_scr[...])
        pv_b, run_b = chunk(pl.multiple_of((qi - 2 - 2 * p) * bq, bq), run_a)
        acc_scr[...] += pv_a + pv_b
        run_scr[...] = run_b
        return 0

    lax.fori_loop(0, qi // 2, chunk_pair, 0)

    @pl.when(qi % 2 == 1)
    def _():
        pv, _ = chunk(0, run_scr[...])
        acc_scr[...] += pv

    o_ref[0] = (_rms(acc_scr[...]) * mn_ref[...]).astype(BF16)


def _suffix_matrix(bk):
    j = jnp.arange(bk)[:, None]
    s = jnp.arange(bk)[None, :]
    half = jnp.concatenate([(j > s).astype(BF16), jnp.ones((bk, HEAD_DIM), BF16)], axis=1)
    return jnp.concatenate([half, half], axis=0)


def _attention(qkv3, mix_gain, heads):
    b, s, _ = qkv3.shape
    bk = HEAD_DIM
    bq = _tile(s, 512)
    assert bq % bk == 0 and s % bq == 0
    kern = functools.partial(_attn_kernel, bq=bq, bk=bk, scale=1.0 / math.sqrt(HEAD_DIM))
    return pl.pallas_call(
        kern,
        grid=(b, heads, s // bq),
        in_specs=[
            pl.BlockSpec((1, bq, HEAD_DIM), lambda bi, h, qi: (bi, qi, h)),
            pl.BlockSpec((1, s, HEAD_DIM), lambda bi, h, qi: (bi, 0, heads + h)),
            pl.BlockSpec((1, s, HEAD_DIM), lambda bi, h, qi: (bi, 0, 2 * heads + h)),
            pl.BlockSpec((2 * bk, bk + HEAD_DIM), lambda bi, h, qi: (0, 0)),
            pl.BlockSpec((1, HEAD_DIM), lambda bi, h, qi: (0, h)),
        ],
        out_specs=pl.BlockSpec((1, bq, HEAD_DIM), lambda bi, h, qi: (bi, qi, h)),
        out_shape=jax.ShapeDtypeStruct((b, s, heads * HEAD_DIM), BF16),
        scratch_shapes=[pltpu.VMEM((bq, HEAD_DIM), F32), pltpu.VMEM((bq, HEAD_DIM), F32)],
        compiler_params=_params(("arbitrary", "arbitrary", "arbitrary")),
        name="stickbreak_attention",
    )(qkv3, qkv3, qkv3, _suffix_matrix(bk), mix_gain)


def _outproj_kernel(yab_ref, yc_ref, x_ref, w_ref, g_ref, wr_ref, br_ref,
                    x1_ref, h2_ref, lg_ref, *, dab, tm, parts):
    rows = tm // parts
    for p in range(parts):
        rs = slice(p * rows, (p + 1) * rows)
        acc = jnp.dot(yab_ref[rs, :], w_ref[0:dab, :], preferred_element_type=F32)
        acc = acc + jnp.dot(yc_ref[rs, :], w_ref[dab:, :], preferred_element_type=F32)
        x1 = x_ref[rs, :] + acc
        x1_ref[rs, :] = x1
        h2 = _rms(x1) * g_ref[...]
        _store_token_major(h2_ref, p * rows, h2)
        lg = lax.dot_general(wr_ref[...], h2.astype(BF16), (((1,), (1,)), ((), ())),
                             preferred_element_type=F32)
        lg_ref[:, rs] = lg + br_ref[...]


def _store_token_major(ref, first_row, val):
    rows, d = val.shape
    rt = d // HEAD_DIM
    for j in range(rt):
        ref[pl.ds(first_row * rt + j, rows, stride=rt), :] = val[:, j * HEAD_DIM:(j + 1) * HEAD_DIM]


def _load_token_major(ref, first_row, rows, rt):
    return [ref[pl.ds(first_row * rt + j, rows, stride=rt), :] for j in range(rt)]


def _out_projection(yab, yc, x, w_bf16, layer, gain, w_route_t, b_route):
    t, d = x.shape
    dab, dc = yab.shape[1], yc.shape[1]
    rt = d // HEAD_DIM
    tm = _tile(t, 512)
    parts = 2 if tm % (2 * HEAD_DIM) == 0 else 1
    kern = functools.partial(_outproj_kernel, dab=dab, tm=tm, parts=parts)
    row = lambda n: pl.BlockSpec((tm, n), lambda i: (i, 0))
    full = lambda shp: pl.BlockSpec(shp, lambda i: (0, 0))
    return pl.pallas_call(
        kern,
        grid=(t // tm,),
        in_specs=[row(dab), row(dc), row(d),
                  pl.BlockSpec((None, dab + dc, d), lambda i: (layer, 0, 0),
                               pipeline_mode=pl.Buffered(1)),
                  full((1, d)),
                  full((ROUTE_ROWS, d)), full((ROUTE_ROWS, 1))],
        out_specs=[row(d), pl.BlockSpec((tm * rt, HEAD_DIM), lambda i: (i, 0)),
                   pl.BlockSpec((ROUTE_ROWS, tm), lambda i: (0, i))],
        out_shape=[jax.ShapeDtypeStruct((t, d), F32), jax.ShapeDtypeStruct((t * rt, HEAD_DIM), F32),
                   jax.ShapeDtypeStruct((ROUTE_ROWS, t), F32)],
        compiler_params=_params(("arbitrary",)),
        name="out_projection",
    )(yab, yc, x, w_bf16, gain, w_route_t, b_route)


def _route_kernel(lg_ref, tri_ref, ones_ref, ltri_ref, dest_ref, gate_ref, meta_ref,
                  cnt_scr, pst_scr, run_scr, *, n_experts, n_groups, row_block, tl):
    ph = pl.program_id(0)
    i = pl.program_id(1)
    epg = n_experts // n_groups

    @pl.when((ph == 0) & (i == 0))
    def _():
        cnt_scr[...] = jnp.zeros_like(cnt_scr)

    lg = lg_ref[...]
    grp = lg[n_experts:n_experts + n_groups, :]
    gmax = jnp.max(grp, axis=0, keepdims=True)
    giota = lax.broadcasted_iota(jnp.int32, grp.shape, 0)
    gidx = jnp.min(jnp.where(grp == gmax, giota, n_groups), axis=0, keepdims=True)
    p_grp = 1.0 / jnp.sum(jnp.exp(grp - gmax), axis=0, keepdims=True)

    sel = lg[0:epg, :]
    for g in range(1, n_groups):
        sel = jnp.where(gidx == g, lg[g * epg:(g + 1) * epg, :], sel)
    eiota = lax.broadcasted_iota(jnp.int32, sel.shape, 0)
    m1 = jnp.max(sel, axis=0, keepdims=True)
    i1 = jnp.min(jnp.where(sel == m1, eiota, epg), axis=0, keepdims=True)
    rest = jnp.where(eiota == i1, -jnp.inf, sel)
    m2 = jnp.max(rest, axis=0, keepdims=True)
    i2 = jnp.min(jnp.where(rest == m2, eiota, epg), axis=0, keepdims=True)
    e = jnp.exp(m2 - m1)
    den = 1.0 + e
    ex1 = gidx * epg + i1
    ex2 = gidx * epg + i2

    xiota = lax.broadcasted_iota(jnp.int32, (n_experts, tl), 0)
    hit1 = xiota == ex1
    hit2 = xiota == ex2
    onehot = (hit1 | hit2).astype(BF16)
    tile_cnt = jnp.dot(onehot, ones_ref[...], preferred_element_type=F32)

    @pl.when(ph == 0)
    def _():
        cnt_scr[...] += tile_cnt

    @pl.when((ph == 1) & (i == 0))
    def _():
        cnt = cnt_scr[...].astype(jnp.int32)
        nblk = (cnt + (row_block - 1)) // row_block
        pst = jnp.dot(ltri_ref[...], nblk.astype(F32).astype(BF16), preferred_element_type=F32)
        pst_scr[...] = pst
        run_scr[...] = jnp.zeros_like(run_scr)
        meta_ref[0:n_experts, :] = pst.astype(jnp.int32)
        meta_ref[n_experts:2 * n_experts, :] = cnt

    @pl.when(ph == 1)
    def _():
        rank = jnp.dot(onehot, tri_ref[...], preferred_element_type=F32)
        base = pst_scr[...] * float(row_block) + run_scr[...]
        slot = jnp.concatenate([base] * (tl // HEAD_DIM), axis=1) + rank
        d1 = jnp.sum(jnp.where(hit1, slot, 0.0), axis=0, keepdims=True)
        d2 = jnp.sum(jnp.where(hit2, slot, 0.0), axis=0, keepdims=True)
        dest_ref[0:1, :] = d1.astype(jnp.int32)
        dest_ref[1:2, :] = d2.astype(jnp.int32)
        gate_ref[0:1, :] = p_grp * (1.0 / den)
        gate_ref[1:2, :] = p_grp * (e / den)
        run_scr[...] += tile_cnt


def _route(logits_t, n_experts, n_groups, row_block):
    t = logits_t.shape[1]
    tl = _tile(t, 512)
    assert tl % HEAD_DIM == 0
    a = jnp.arange(tl)
    tri = (a[:, None] < a[None, :]).astype(BF16)
    ones = jnp.ones((tl, HEAD_DIM), BF16)
    ea = jnp.arange(n_experts)
    ltri = (ea[None, :] < ea[:, None]).astype(BF16)
    kern = functools.partial(_route_kernel, n_experts=n_experts, n_groups=n_groups,
                             row_block=row_block, tl=tl)
    full = lambda shp: pl.BlockSpec(shp, lambda p, i: (0, 0))
    return pl.pallas_call(
        kern,
        grid=(2, t // tl),
        in_specs=[pl.BlockSpec((ROUTE_ROWS, tl), lambda p, i: (0, i)),
                  full((tl, tl)), full((tl, HEAD_DIM)), full((n_experts, n_experts))],
        out_specs=[pl.BlockSpec((2, tl), lambda p, i: (0, i * p)),
                   pl.BlockSpec((2, tl), lambda p, i: (0, i * p)),
                   full((2 * n_experts, HEAD_DIM))],
        out_shape=[jax.ShapeDtypeStruct((2, t), jnp.int32), jax.ShapeDtypeStruct((2, t), F32),
                   jax.ShapeDtypeStruct((2 * n_experts, HEAD_DIM), jnp.int32)],
        scratch_shapes=[pltpu.VMEM((n_experts, HEAD_DIM), F32),
                        pltpu.VMEM((n_experts, HEAD_DIM), F32),
                        pltpu.VMEM((n_experts, HEAD_DIM), F32)],
        compiler_params=_params(("arbitrary", "arbitrary")),
        name="moe_route",
    )(logits_t, tri, ones, ltri)


PAD_ROW = -1
DMA_PRIORITIES = 2


def _tables_kernel(dest_ref, meta_ref, src_ref, bexp_ref, nused_ref, *,
                   t, n_experts, n_blocks, row_block):
    def mark_pad(ri, _):
        src_ref[ri] = PAD_ROW
        return 0

    lax.fori_loop(0, n_blocks * row_block, mark_pad, 0, unroll=16)

    def fill(tok, _):
        src_ref[dest_ref[tok]] = tok
        src_ref[dest_ref[t + tok]] = t + tok
        return 0

    lax.fori_loop(0, t, fill, 0, unroll=8)

    def clear(bi, _):
        bexp_ref[bi] = n_experts - 1
        return 0

    lax.fori_loop(0, n_blocks, clear, 0)

    def per_expert(ex, used):
        first = meta_ref[ex]
        nblk = (meta_ref[n_experts + ex] + (row_block - 1)) // row_block

        def per_block(bj, _):
            bexp_ref[first + bj] = ex
            return 0

        lax.fori_loop(0, nblk, per_block, 0)
        return first + nblk

    nused_ref[0] = lax.fori_loop(0, n_experts, per_expert, 0)


def _tables(dest_flat, meta_flat, t, n_experts, n_blocks, row_block):
    kern = functools.partial(_tables_kernel, t=t, n_experts=n_experts, n_blocks=n_blocks,
                             row_block=row_block)
    smem = pl.BlockSpec(memory_space=pltpu.SMEM)
    return pl.pallas_call(
        kern,
        in_specs=[smem, smem],
        out_specs=[smem, smem, smem],
        out_shape=[jax.ShapeDtypeStruct((n_blocks * row_block,), jnp.int32),
                   jax.ShapeDtypeStruct((n_blocks,), jnp.int32),
                   jax.ShapeDtypeStruct((1,), jnp.int32)],
        name="moe_tables",
    )(dest_flat, meta_flat)


def _expert_kernel(bexp_ref, nused_ref, src_ref, h2_hbm, wg_ref, wu_ref, wd_ref,
                   y_hbm, xbuf, ybuf, wg_bf, wu_bf, wd_bf, gsem, ssem, *, t, rt, row_block, n_blocks):
    i = pl.program_id(0)
    n_used = nused_ref[0]
    slot = i % 2
    buf_rows = row_block * rt

    def buf(ref, s):
        return ref.at[pl.ds(pl.multiple_of(s * buf_rows, buf_rows), buf_rows), :]

    def start_gather(blk, s):
        def body(rp, _):
            for prio in range(DMA_PRIORITIES):
                r = rp * DMA_PRIORITIES + prio
                src = src_ref[blk * row_block + r]
                if t & (t - 1) == 0:
                    tok = src & (t - 1)
                else:
                    tok = jnp.where(src < 0, 0, jnp.where(src >= t, src - t, src))
                pltpu.make_async_copy(h2_hbm.at[pl.ds(pl.multiple_of(tok * rt, rt), rt), :],
                                      xbuf.at[pl.ds(pl.multiple_of((s * row_block + r) * rt, rt), rt), :],
                                      gsem.at[s]).start(priority=prio)
            return 0
        lax.fori_loop(0, row_block // DMA_PRIORITIES, body, 0, unroll=4)

    def start_scatter(blk, s):
        def body(rp, _):
            for prio in range(DMA_PRIORITIES):
                r = rp * DMA_PRIORITIES + prio
                src = src_ref[blk * row_block + r]
                dst = jnp.where(src < 0, 2 * t + s * row_block + r, src)
                pltpu.make_async_copy(ybuf.at[pl.ds(pl.multiple_of((s * row_block + r) * rt, rt), rt), :],
                                      y_hbm.at[pl.ds(pl.multiple_of(dst * rt, rt), rt), :],
                                      ssem.at[s]).start(priority=prio)
            return 0
        lax.fori_loop(0, row_block // DMA_PRIORITIES, body, 0, unroll=4)

    def wait_gather(s):
        pltpu.make_async_copy(buf(xbuf, s), buf(xbuf, s), gsem.at[s]).wait()

    def wait_scatter(s):
        pltpu.make_async_copy(buf(ybuf, s), buf(ybuf, s), ssem.at[s]).wait()

    @pl.when(i == 0)
    def _():
        ybuf[...] = jnp.zeros_like(ybuf)
        for s in range(2):
            spare = y_hbm.at[pl.ds((2 * t + s * row_block) * rt, buf_rows), :]
            pltpu.make_async_copy(buf(ybuf, s), spare, ssem.at[s]).start()
        for s in range(2):
            wait_scatter(s)
        start_gather(0, 0)

    @pl.when((i == 0) | (bexp_ref[i] != bexp_ref[jnp.maximum(i - 1, 0)]))
    def _():
        wg_bf[...] = wg_ref[...].astype(BF16)
        wu_bf[...] = wu_ref[...].astype(BF16)
        wd_bf[...] = wd_ref[...].astype(BF16)

    @pl.when(i + 1 < n_used)
    def _():
        start_gather(i + 1, 1 - slot)

    @pl.when(i < n_used)
    def _():
        wait_gather(slot)

        @pl.when(i >= 2)
        def _():
            wait_scatter(slot)

        x = jnp.concatenate(_load_token_major(xbuf, slot * row_block, row_block, rt), axis=1).astype(BF16)
        gt = jnp.dot(x, wg_bf[...], preferred_element_type=F32)
        up = jnp.dot(x, wu_bf[...], preferred_element_type=F32)
        hid = (jax.nn.silu(gt) * up).astype(BF16)
        _store_token_major(ybuf, slot * row_block, jnp.dot(hid, wd_bf[...], preferred_element_type=F32))
        start_scatter(i, slot)

    @pl.when(i == n_blocks - 1)
    def _():
        wait_scatter((n_used - 1) % 2)

        @pl.when(n_used >= 2)
        def _():
            wait_scatter(n_used % 2)


def _experts(h2_tm, t, w_gate, w_up, w_down, layer, bexp, nused, src, row_block):
    rt = h2_tm.shape[0] // t
    d = rt * HEAD_DIM
    n_blocks = bexp.shape[0]
    de = w_gate.shape[3]
    kern = functools.partial(_expert_kernel, t=t, rt=rt, row_block=row_block, n_blocks=n_blocks)
    grid_spec = pltpu.PrefetchScalarGridSpec(
        num_scalar_prefetch=3,
        grid=(n_blocks,),
        in_specs=[
            pl.BlockSpec(memory_space=pl.ANY),
            pl.BlockSpec((None, None, d, de), lambda i, be, nu, sr: (layer, be[i], 0, 0)),
            pl.BlockSpec((None, None, d, de), lambda i, be, nu, sr: (layer, be[i], 0, 0)),
            pl.BlockSpec((None, None, de, d), lambda i, be, nu, sr: (layer, be[i], 0, 0)),
        ],
        out_specs=pl.BlockSpec(memory_space=pl.ANY),
        scratch_shapes=[
            pltpu.VMEM((2 * row_block * rt, HEAD_DIM), F32),
            pltpu.VMEM((2 * row_block * rt, HEAD_DIM), F32),
            pltpu.VMEM((d, de), BF16),
            pltpu.VMEM((d, de), BF16),
            pltpu.VMEM((de, d), BF16),
            pltpu.SemaphoreType.DMA((2,)),
            pltpu.SemaphoreType.DMA((2,)),
        ],
    )
    return pl.pallas_call(
        kern,
        grid_spec=grid_spec,
        out_shape=jax.ShapeDtypeStruct(((2 * t + 2 * row_block) * rt, HEAD_DIM), F32),
        compiler_params=_params(("arbitrary",)),
        name="moe_experts",
    )(bexp, nused, src, h2_tm, w_gate, w_up, w_down)


def _combine_kernel(x_ref, y0_ref, y1_ref, g_ref, o_ref, *, tm, rt):
    g = g_ref[...]
    g0 = jnp.broadcast_to(g[:, 0:1], (tm, HEAD_DIM))
    g1 = jnp.broadcast_to(g[:, 1:2], (tm, HEAD_DIM))
    for j in range(rt):
        sl = slice(j * HEAD_DIM, (j + 1) * HEAD_DIM)
        y0 = y0_ref[pl.ds(j, tm, stride=rt), :]
        y1 = y1_ref[pl.ds(j, tm, stride=rt), :]
        o_ref[:, sl] = x_ref[:, sl] + (y0 * g0 + y1 * g1)


def _combine(x1, y_tok, gates_t):
    t, d = x1.shape
    rt = d // HEAD_DIM
    tm = _tile(t, 512)
    nb = t // tm
    return pl.pallas_call(
        functools.partial(_combine_kernel, tm=tm, rt=rt),
        grid=(nb,),
        in_specs=[pl.BlockSpec((tm, d), lambda i: (i, 0)),
                  pl.BlockSpec((tm * rt, HEAD_DIM), lambda i: (i, 0)),
                  pl.BlockSpec((tm * rt, HEAD_DIM), lambda i: (i + nb, 0)),
                  pl.BlockSpec((tm, 2), lambda i: (i, 0))],
        out_specs=pl.BlockSpec((tm, d), lambda i: (i, 0)),
        out_shape=jax.ShapeDtypeStruct((t, d), F32),
        compiler_params=_params(("arbitrary",)),
        name="moe_combine",
    )(x1, y_tok, y_tok, gates_t)


MOE_ROW_BLOCK = 256


def kernel(x, norm_mix, w_in, conv_a_w, conv_a_b, lru_wa, lru_ba, lru_wx, lru_bx, lru_lam,
           conv_b_w, q_norm, k_norm, mix_norm, w_out, norm_ffn, w_router_group, b_router_group,
           w_router_expert, b_router_expert, w_gate, w_up, w_down):
    bsz, s, d = x.shape
    t = bsz * s
    depth = w_in.shape[0]
    lru_heads = lru_wa.shape[1]
    d_lru = lru_heads * HEAD_DIM
    d_conv = conv_b_w.shape[-1]
    d_sb =(w_in.shape[-1] - 2 * d_lru - 3 * d_conv) // 3
    sb_heads = d_sb // HEAD_DIM
    n_groups = w_router_group.shape[-1]
    n_experts = w_router_expert.shape[-1]
    assert n_experts // n_groups == EXPERTS_PER_GROUP and n_experts + n_groups <= ROUTE_ROWS
    row_block = MOE_ROW_BLOCK
    n_blocks = -(-(2 * t) // row_block) + n_experts

    xt = x.reshape(t, d)
    w_in_bf = w_in.astype(BF16)
    w_out_bf = w_out.astype(BF16)
    for l in range(depth):
        row = lambda v: v[l].reshape(1, -1)
        mix, qkv = _in_projection(xt, row(norm_mix), w_in_bf, l, row(q_norm), row(k_norm), d_sb)
        mix3 = mix.reshape(bsz, s, -1)
        qkv3 = qkv.reshape(bsz, s, -1)
        mg = row(mix_norm)
        w_gates = jnp.concatenate([lru_wa[l], lru_wx[l]], axis=-1).astype(BF16)
        yab = _mixers(mix3, conv_a_w[l], row(conv_a_b), w_gates, row(lru_ba), row(lru_bx),
                      row(lru_lam), conv_b_w[l], mg[:, :d_lru + d_conv])
        yc = _attention(qkv3, mg[:, d_lru + d_conv:], sb_heads)

        pad = jnp.zeros((d, ROUTE_ROWS - n_experts - n_groups), F32)
        w_route_t = jnp.concatenate([w_router_expert[l], w_router_group[l], pad], axis=1).T.astype(BF16)
        b_route = jnp.concatenate([b_router_expert[l], b_router_group[l],
                                   jnp.zeros((ROUTE_ROWS - n_experts - n_groups,), F32)]).reshape(-1, 1)
        x1, h2, logits_t = _out_projection(yab.reshape(t, -1), yc.reshape(t, -1), xt,
                                           w_out_bf, l, row(norm_ffn), w_route_t, b_route)

        dest, gates, meta = _route(logits_t, n_experts, n_groups, row_block)
        src, bexp, nused = _tables(dest.reshape(-1), meta[:, 0], t, n_experts, n_blocks, row_block)
        y_tok = _experts(h2, t, w_gate, w_up, w_down, l, bexp, nused, src, row_block)
        xt = _combine(x1, y_tok, gates.T)
    return xt.reshape(bsz, s, d)
```

```python
import functools
import math

import jax
import jax.numpy as jnp
from jax import lax
from jax.experimental import pallas as pl
from jax.experimental.pallas import tpu as pltpu

EPS = 1e-6
LRU_C = 8.0
LOG2_E = 1.4426950408889634
HEAD_DIM = 128
SUBLANES = 8
EXPERTS_PER_GROUP = 8
ROUTE_ROWS = 48
VMEM_LIMIT = 56 * 1024 * 1024

F32 = jnp.float32
BF16 = jnp.bfloat16


def _tile(n, pref):
    if n <= pref:
        return n
    t = pref
    while n % t:
        t -= SUBLANES
    assert t > 0
    return t


def _params(sem):
    return pltpu.CompilerParams(dimension_semantics=sem, vmem_limit_bytes=VMEM_LIMIT)


def _rms(v):
    return v * lax.rsqrt(jnp.mean(v * v, axis=-1, keepdims=True) + EPS)


def _inproj_kernel(x_ref, g_ref, w_ref, qn_ref, kn_ref, mix_ref, qkv_ref, h_scr, *, n_mix, heads):
    j = pl.program_id(1)

    @pl.when(j == 0)
    def _():
        h_scr[...] = (_rms(x_ref[...]) * g_ref[...]).astype(BF16)

    acc = jnp.dot(h_scr[...], w_ref[...], preferred_element_type=F32)

    @pl.when(j < n_mix)
    def _():
        mix_ref[...] = acc

    def head_norm(gain_ref):
        for h in range(heads):
            sl = slice(h * HEAD_DIM, (h + 1) * HEAD_DIM)
            qkv_ref[:, sl] = (_rms(acc[:, sl]) * gain_ref[...]).astype(BF16)

    @pl.when(j == n_mix)
    def _():
        head_norm(qn_ref)

    @pl.when(j == n_mix + 1)
    def _():
        head_norm(kn_ref)

    @pl.when(j == n_mix + 2)
    def _():
        qkv_ref[...] = acc.astype(BF16)


def _in_projection(x, gain, w_bf16, layer, q_gain, k_gain, d_sb):
    t, d = x.shape
    d_in = w_bf16.shape[2]
    tn = d_sb
    assert d_in % tn == 0 and (d_in - 3 * d_sb) % tn == 0
    n_mix = (d_in - 3 * d_sb) // tn
    tm = _tile(t, 1024)
    kern = functools.partial(_inproj_kernel, n_mix=n_mix, heads=d_sb // HEAD_DIM)
    return pl.pallas_call(
        kern,
        grid=(t // tm, d_in // tn),
        in_specs=[
            pl.BlockSpec((tm, d), lambda i, j: (i, 0)),
            pl.BlockSpec((1, d), lambda i, j: (0, 0)),
            pl.BlockSpec((None, d, tn), lambda i, j: (layer, 0, j)),
            pl.BlockSpec((1, HEAD_DIM), lambda i, j: (0, 0)),
            pl.BlockSpec((1, HEAD_DIM), lambda i, j: (0, 0)),
        ],
        out_specs=[
            pl.BlockSpec((tm, tn), lambda i, j: (i, jnp.minimum(j, n_mix - 1))),
            pl.BlockSpec((tm, tn), lambda i, j: (i, jnp.maximum(j - n_mix, 0))),
        ],
        out_shape=[
            jax.ShapeDtypeStruct((t, n_mix * tn), F32),
            jax.ShapeDtypeStruct((t, 3 * d_sb), BF16),
        ],
        scratch_shapes=[pltpu.VMEM((tm, d), BF16)],
        compiler_params=_params(("arbitrary", "arbitrary")),
        name="in_projection",
    )(x, gain, w_bf16, q_gain, k_gain)


def _softplus(v):
    return jnp.maximum(v, 0.0) + jnp.log1p(jnp.exp(-jnp.abs(v)))


def _sigmoid(v):
    return 0.5 * (jnp.tanh(0.5 * v) + 1.0)


def _causal_conv(ext_scr, cols, new_rows, cw, ts):
    taps = cw.shape[0]
    ext_scr[SUBLANES:SUBLANES + ts, cols] = new_rows
    out = None
    for k in range(taps):
        lo = SUBLANES - (taps - 1) + k
        term = cw[k:k + 1, :] * ext_scr[lo:lo + ts, cols]
        out = term if out is None else out + term
    ext_scr[0:SUBLANES, cols] = ext_scr[ts:ts + SUBLANES, cols]
    return out


def _mixers_kernel(mix_ref, cwa_ref, cba_ref, wg_ref, ba_ref, bx_ref, lam_ref, cwb_ref, mn_ref,
                   out_ref, exta_scr, extb_scr, a_scr, u_scr, hc_scr, *, ts, heads, groups):
    si = pl.program_id(1)
    d_lru = heads * HEAD_DIM
    d_conv = groups * HEAD_DIM

    @pl.when(si == 0)
    def _():
        exta_scr[0:SUBLANES, :] = jnp.zeros((SUBLANES, d_lru), F32)
        extb_scr[0:SUBLANES, :] = jnp.zeros((SUBLANES, d_conv), F32)
        hc_scr[...] = jnp.zeros_like(hc_scr)

    sub = lax.broadcasted_iota(jnp.int32, (ts, HEAD_DIM), 0) & (SUBLANES - 1)

    for h in range(heads):
        cols = slice(h * HEAD_DIM, (h + 1) * HEAD_DIM)
        xc = _causal_conv(exta_scr, cols, mix_ref[0, :, cols], cwa_ref[:, cols], ts) + cba_ref[:, cols]
        gates = jnp.dot(xc.astype(BF16), wg_ref[h], preferred_element_type=F32)
        r = _sigmoid(gates[:, :HEAD_DIM] + ba_ref[:, cols])
        i = _sigmoid(gates[:, HEAD_DIM:] + bx_ref[:, cols])
        log_a = (-LRU_C) * r * _softplus(-lam_ref[:, cols])
        a = jnp.exp(log_a)
        u = jnp.sqrt(-jnp.tanh(log_a) * (a * a + 1.0)) * (i * xc)

        d = 1
        while d < SUBLANES:
            keep = sub >= d
            a_sh = jnp.where(keep, pltpu.roll(a, d, 0), 1.0)
            u_sh = jnp.where(keep, pltpu.roll(u, d, 0), 0.0)
            u = a * u_sh + u
            a = a * a_sh
            d *= 2
        a_scr[...] = a
        u_scr[...] = u

        def carry_step(c, hb):
            off = pl.multiple_of(c * SUBLANES, SUBLANES)
            hrow = u_scr[pl.ds(off, SUBLANES), :] + a_scr[pl.ds(off, SUBLANES), :] * hb
            u_scr[pl.ds(off, SUBLANES), :] = hrow
            return jnp.broadcast_to(hrow[SUBLANES - 1:SUBLANES, :], (SUBLANES, HEAD_DIM))

        hc_scr[:, cols] = lax.fori_loop(0, ts // SUBLANES, carry_step, hc_scr[:, cols], unroll=8)

        y = u_scr[...] * jax.nn.gelu(mix_ref[0, :, d_lru + h * HEAD_DIM:d_lru + (h + 1) * HEAD_DIM])
        out_ref[0, :, cols] = (_rms(y) * mn_ref[:, cols]).astype(BF16)

    for g in range(groups):
        cols = slice(g * HEAD_DIM, (g + 1) * HEAD_DIM)
        col = lambda part: slice(2 * d_lru + part * d_conv + g * HEAD_DIM,
                                 2 * d_lru + part * d_conv + (g + 1) * HEAD_DIM)
        conv = _causal_conv(extb_scr, cols, mix_ref[0, :, col(1)] * mix_ref[0, :, col(2)], cwb_ref[:, cols], ts)
        y = mix_ref[0, :, col(0)] * conv
        ocols = slice(d_lru + g * HEAD_DIM, d_lru + (g + 1) * HEAD_DIM)
        out_ref[0, :, ocols] = (_rms(y) * mn_ref[:, ocols]).astype(BF16)


def _mixers(mix3, conv_a_w, conv_a_b, w_gates, b_a, b_x, lam, conv_b_w, mix_gain):
    b, s, d_mix = mix3.shape
    heads = w_gates.shape[0]
    d_lru = heads * HEAD_DIM
    d_conv = conv_b_w.shape[1]
    groups = d_conv // HEAD_DIM
    assert d_mix == 2 * d_lru + 3 * d_conv
    ts = _tile(s, 512)
    assert max(conv_a_w.shape[0], conv_b_w.shape[0]) - 1 <= SUBLANES and ts % SUBLANES == 0
    kern = functools.partial(_mixers_kernel, ts=ts, heads=heads, groups=groups)
    full = lambda a: pl.BlockSpec(a.shape, lambda bi, si: (0,) * a.ndim)
    return pl.pallas_call(
        kern,
        grid=(b, s // ts),
        in_specs=[pl.BlockSpec((1, ts, d_mix), lambda bi, si: (bi, si, 0)),
                  full(conv_a_w), full(conv_a_b), full(w_gates), full(b_a), full(b_x), full(lam),
                  full(conv_b_w), full(mix_gain)],
        out_specs=pl.BlockSpec((1, ts, d_lru + d_conv), lambda bi, si: (bi, si, 0)),
        out_shape=jax.ShapeDtypeStruct((b, s, d_lru + d_conv), BF16),
        scratch_shapes=[
            pltpu.VMEM((ts + SUBLANES, d_lru), F32),
            pltpu.VMEM((ts + SUBLANES, d_conv), F32),
            pltpu.VMEM((ts, HEAD_DIM), F32),
            pltpu.VMEM((ts, HEAD_DIM), F32),
            pltpu.VMEM((SUBLANES, d_lru), F32),
        ],
        compiler_params=_params(("arbitrary", "arbitrary")),
        name="mixers_rglru_conv",
    )(mix3, conv_a_w, conv_a_b, w_gates, b_a, b_x, lam, conv_b_w, mix_gain)


def _attn_kernel(q_ref, k_ref, v_ref, m_ref, mn_ref, o_ref, acc_scr, run_scr, *, bq, bk, scale):
    qi = pl.program_id(2)
    q = q_ref[0]
    n_sub = bq // bk

    def log_sigmoids(s):
        z = s * scale
        ls = jnp.minimum(z, 0.0) - jnp.log(1.0 + jnp.exp2(jnp.abs(s) * (-scale * LOG2_E)))
        return ls, ls - z

    def suffix_sums(lk):
        hi = lk.astype(BF16)
        lo = (lk - hi.astype(F32)).astype(BF16)
        return jnp.dot(jnp.concatenate([hi, lo], axis=1), m_ref[...], preferred_element_type=F32)

    def chunk(c0, run):
        kc = k_ref[0, pl.ds(c0, bq), :]
        vc = v_ref[0, pl.ds(c0, bq), :]
        ls, lk = log_sigmoids(lax.dot_general(q, kc, (((1,), (1,)), ((), ())), preferred_element_type=F32))
        ws = [None] * n_sub
        for k in reversed(range(n_sub)):
            sl = slice(k * bk, (k + 1) * bk)
            sums = suffix_sums(lk[:, sl])
            ws[k] = jnp.exp(ls[:, sl] + sums[:, :bk] + run).astype(BF16)
            run = run + sums[:, bk:]
        pv = jnp.dot(jnp.concatenate(ws, axis=1), vc, preferred_element_type=F32)
        return pv, run

    def diagonal_chunk(c0):
        kc = k_ref[0, pl.ds(c0, bq), :]
        vc = v_ref[0, pl.ds(c0, bq), :]
        s = lax.dot_general(q, kc, (((1,), (1,)), ((), ())), preferred_element_type=F32)
        run = jnp.zeros((bq, HEAD_DIM), F32)
        ws = [None] * n_sub
        for k in reversed(range(n_sub)):
            r0 = k * bk
            rows = bq - r0
            ls, lk = log_sigmoids(s[r0:, r0:r0 + bk])
            mask = (lax.broadcasted_iota(jnp.int32, (rows, bk), 1)
                    < lax.broadcasted_iota(jnp.int32, (rows, bk), 0))
            sums = suffix_sums(jnp.where(mask, lk, 0.0))
            w = jnp.where(mask, jnp.exp(ls + sums[:, :bk] + run[r0:]), 0.0).astype(BF16)
            seen = run[r0:] + sums[:, bk:]
            if r0:
                w = jnp.concatenate([jnp.zeros((r0, bk), BF16), w], axis=0)
                seen = jnp.concatenate([run[:r0], seen], axis=0)
            ws[k] = w
            run = seen
        pv = jnp.dot(jnp.concatenate(ws, axis=1), vc, preferred_element_type=F32)
        return pv, run

    acc, run = diagonal_chunk(pl.multiple_of(qi * bq, bq))
    acc_scr[...] = acc
    run_scr[...] = run

    def chunk_pair(p, _):
        pv_a, run_a = chunk(pl.multiple_of((qi - 1 - 2 * p) * bq, bq), run_scr[...])
        pv_b, run_b = chunk(pl.multiple_of((qi - 2 - 2 * p) * bq, bq), run_a)
        acc_scr[...] += pv_a + pv_b
        run_scr[...] = run_b
        return 0

    lax.fori_loop(0, qi // 2, chunk_pair, 0)

    @pl.when(qi % 2 == 1)
    def _():
        pv, _ = chunk(0, run_scr[...])
        acc_scr[...] += pv

    o_ref[0] = (_rms(acc_scr[...]) * mn_ref[...]).astype(BF16)


def _suffix_matrix(bk):
    j = jnp.arange(bk)[:, None]
    s = jnp.arange(bk)[None, :]
    half = jnp.concatenate([(j > s).astype(BF16), jnp.ones((bk, HEAD_DIM), BF16)], axis=1)
    return jnp.concatenate([half, half], axis=0)


def _attention(qkv3, mix_gain, heads):
    b, s, _ = qkv3.shape
    bk = HEAD_DIM
    bq = _tile(s, 512)
    assert bq % bk == 0 and s % bq == 0
    kern = functools.partial(_attn_kernel, bq=bq, bk=bk, scale=1.0 / math.sqrt(HEAD_DIM))
    return pl.pallas_call(
        kern,
        grid=(b, heads, s // bq),
        in_specs=[
            pl.BlockSpec((1, bq, HEAD_DIM), lambda bi, h, qi: (bi, qi, h)),
            pl.BlockSpec((1, s, HEAD_DIM), lambda bi, h, qi: (bi, 0, heads + h)),
            pl.BlockSpec((1, s, HEAD_DIM), lambda bi, h, qi: (bi, 0, 2 * heads + h)),
            pl.BlockSpec((2 * bk, bk + HEAD_DIM), lambda bi, h, qi: (0, 0)),
            pl.BlockSpec((1, HEAD_DIM), lambda bi, h, qi: (0, h)),
        ],
        out_specs=pl.BlockSpec((1, bq, HEAD_DIM), lambda bi, h, qi: (bi, qi, h)),
        out_shape=jax.ShapeDtypeStruct((b, s, heads * HEAD_DIM), BF16),
        scratch_shapes=[pltpu.VMEM((bq, HEAD_DIM), F32), pltpu.VMEM((bq, HEAD_DIM), F32)],
        compiler_params=_params(("arbitrary", "arbitrary", "arbitrary")),
        name="stickbreak_attention",
    )(qkv3, qkv3, qkv3, _suffix_matrix(bk), mix_gain)


def _outproj_kernel(yab_ref, yc_ref, x_ref, w_ref, g_ref, wr_ref, br_ref,
                    x1_ref, h2_ref, lg_ref, *, dab, tm, parts):
    rows = tm // parts
    for p in range(parts):
        rs = slice(p * rows, (p + 1) * rows)
        acc = jnp.dot(yab_ref[rs, :], w_ref[0:dab, :], preferred_element_type=F32)
        acc = acc + jnp.dot(yc_ref[rs, :], w_ref[dab:, :], preferred_element_type=F32)
        x1 = x_ref[rs, :] + acc
        x1_ref[rs, :] = x1
        h2 = _rms(x1) * g_ref[...]
        _store_token_major(h2_ref, p * rows, h2)
        lg = lax.dot_general(wr_ref[...], h2.astype(BF16), (((1,), (1,)), ((), ())),
                             preferred_element_type=F32)
        lg_ref[:, rs] = lg + br_ref[...]


def _store_token_major(ref, first_row, val):
    rows, d = val.shape
    rt = d // HEAD_DIM
    for j in range(rt):
        ref[pl.ds(first_row * rt + j, rows, stride=rt), :] = val[:, j * HEAD_DIM:(j + 1) * HEAD_DIM]


def _load_token_major(ref, first_row, rows, rt):
    return [ref[pl.ds(first_row * rt + j, rows, stride=rt), :] for j in range(rt)]


def _out_projection(yab, yc, x, w_bf16, layer, gain, w_route_t, b_route):
    t, d = x.shape
    dab, dc = yab.shape[1], yc.shape[1]
    rt = d // HEAD_DIM
    tm = _tile(t, 512)
    parts = 2 if tm % (2 * HEAD_DIM) == 0 else 1
    kern = functools.partial(_outproj_kernel, dab=dab, tm=tm, parts=parts)
    row = lambda n: pl.BlockSpec((tm, n), lambda i: (i, 0))
    full = lambda shp: pl.BlockSpec(shp, lambda i: (0, 0))
    return pl.pallas_call(
        kern,
        grid=(t // tm,),
        in_specs=[row(dab), row(dc), row(d),
                  pl.BlockSpec((None, dab + dc, d), lambda i: (layer, 0, 0),
                               pipeline_mode=pl.Buffered(1)),
                  full((1, d)),
                  full((ROUTE_ROWS, d)), full((ROUTE_ROWS, 1))],
        out_specs=[row(d), pl.BlockSpec((tm * rt, HEAD_DIM), lambda i: (i, 0)),
                   pl.BlockSpec((ROUTE_ROWS, tm), lambda i: (0, i))],
        out_shape=[jax.ShapeDtypeStruct((t, d), F32), jax.ShapeDtypeStruct((t * rt, HEAD_DIM), F32),
                   jax.ShapeDtypeStruct((ROUTE_ROWS, t), F32)],
        compiler_params=_params(("arbitrary",)),
        name="out_projection",
    )(yab, yc, x, w_bf16, gain, w_route_t, b_route)


def _route_kernel(lg_ref, tri_ref, ones_ref, ltri_ref, dest_ref, gate_ref, meta_ref,
                  cnt_scr, pst_scr, run_scr, *, n_experts, n_groups, row_block, tl):
    ph = pl.program_id(0)
    i = pl.program_id(1)
    epg = n_experts // n_groups

    @pl.when((ph == 0) & (i == 0))
    def _():
        cnt_scr[...] = jnp.zeros_like(cnt_scr)

    lg = lg_ref[...]
    grp = lg[n_experts:n_experts + n_groups, :]
    gmax = jnp.max(grp, axis=0, keepdims=True)
    giota = lax.broadcasted_iota(jnp.int32, grp.shape, 0)
    gidx = jnp.min(jnp.where(grp == gmax, giota, n_groups), axis=0, keepdims=True)
    p_grp = 1.0 / jnp.sum(jnp.exp(grp - gmax), axis=0, keepdims=True)

    sel = lg[0:epg, :]
    for g in range(1, n_groups):
        sel = jnp.where(gidx == g, lg[g * epg:(g + 1) * epg, :], sel)
    eiota = lax.broadcasted_iota(jnp.int32, sel.shape, 0)
    m1 = jnp.max(sel, axis=0, keepdims=True)
    i1 = jnp.min(jnp.where(sel == m1, eiota, epg), axis=0, keepdims=True)
    rest = jnp.where(eiota == i1, -jnp.inf, sel)
    m2 = jnp.max(rest, axis=0, keepdims=True)
    i2 = jnp.min(jnp.where(rest == m2, eiota, epg), axis=0, keepdims=True)
    e = jnp.exp(m2 - m1)
    den = 1.0 + e
    ex1 = gidx * epg + i1
    ex2 = gidx * epg + i2

    xiota = lax.broadcasted_iota(jnp.int32, (n_experts, tl), 0)
    hit1 = xiota == ex1
    hit2 = xiota == ex2
    onehot = (hit1 | hit2).astype(BF16)
    tile_cnt = jnp.dot(onehot, ones_ref[...], preferred_element_type=F32)

    @pl.when(ph == 0)
    def _():
        cnt_scr[...] += tile_cnt

    @pl.when((ph == 1) & (i == 0))
    def _():
        cnt = cnt_scr[...].astype(jnp.int32)
        nblk = (cnt + (row_block - 1)) // row_block
        pst = jnp.dot(ltri_ref[...], nblk.astype(F32).astype(BF16), preferred_element_type=F32)
        pst_scr[...] = pst
        run_scr[...] = jnp.zeros_like(run_scr)
        meta_ref[0:n_experts, :] = pst.astype(jnp.int32)
        meta_ref[n_experts:2 * n_experts, :] = cnt

    @pl.when(ph == 1)
    def _():
        rank = jnp.dot(onehot, tri_ref[...], preferred_element_type=F32)
        base = pst_scr[...] * float(row_block) + run_scr[...]
        slot = jnp.concatenate([base] * (tl // HEAD_DIM), axis=1) + rank
        d1 = jnp.sum(jnp.where(hit1, slot, 0.0), axis=0, keepdims=True)
        d2 = jnp.sum(jnp.where(hit2, slot, 0.0), axis=0, keepdims=True)
        dest_ref[0:1, :] = d1.astype(jnp.int32)
        dest_ref[1:2, :] = d2.astype(jnp.int32)
        gate_ref[0:1, :] = p_grp * (1.0 / den)
        gate_ref[1:2, :] = p_grp * (e / den)
        run_scr[...] += tile_cnt


def _route(logits_t, n_experts, n_groups, row_block):
    t = logits_t.shape[1]
    tl = _tile(t, 512)
    assert tl % HEAD_DIM == 0
    a = jnp.arange(tl)
    tri = (a[:, None] < a[None, :]).astype(BF16)
    ones = jnp.ones((tl, HEAD_DIM), BF16)
    ea = jnp.arange(n_experts)
    ltri = (ea[None, :] < ea[:, None]).astype(BF16)
    kern = functools.partial(_route_kernel, n_experts=n_experts, n_groups=n_groups,
                             row_block=row_block, tl=tl)
    full = lambda shp: pl.BlockSpec(shp, lambda p, i: (0, 0))
    return pl.pallas_call(
        kern,
        grid=(2, t // tl),
        in_specs=[pl.BlockSpec((ROUTE_ROWS, tl), lambda p, i: (0, i)),
                  full((tl, tl)), full((tl, HEAD_DIM)), full((n_experts, n_experts))],
        out_specs=[pl.BlockSpec((2, tl), lambda p, i: (0, i * p)),
                   pl.BlockSpec((2, tl), lambda p, i: (0, i * p)),
                   full((2 * n_experts, HEAD_DIM))],
        out_shape=[jax.ShapeDtypeStruct((2, t), jnp.int32), jax.ShapeDtypeStruct((2, t), F32),
                   jax.ShapeDtypeStruct((2 * n_experts, HEAD_DIM), jnp.int32)],
        scratch_shapes=[pltpu.VMEM((n_experts, HEAD_DIM), F32),
                        pltpu.VMEM((n_experts, HEAD_DIM), F32),
                        pltpu.VMEM((n_experts, HEAD_DIM), F32)],
        compiler_params=_params(("arbitrary", "arbitrary")),
        name="moe_route",
    )(logits_t, tri, ones, ltri)


PAD_ROW = -1
DMA_PRIORITIES = 2
X_BUFFERS = 2
Y_BUFFERS = 3


def _tables_kernel(dest_ref, meta_ref, src_ref, bexp_ref, nused_ref, *,
                   t, n_experts, n_blocks, row_block):
    def mark_pad(ri, _):
        src_ref[ri] = PAD_ROW
        return 0

    lax.fori_loop(0, n_blocks * row_block, mark_pad, 0, unroll=16)

    def fill(tok, _):
        src_ref[dest_ref[tok]] = tok
        src_ref[dest_ref[t + tok]] = t + tok
        return 0

    lax.fori_loop(0, t, fill, 0, unroll=8)

    def clear(bi, _):
        bexp_ref[bi] = n_experts - 1
        return 0

    lax.fori_loop(0, n_blocks, clear, 0)

    def per_expert(ex, used):
        first = meta_ref[ex]
        nblk = (meta_ref[n_experts + ex] + (row_block - 1)) // row_block

        def per_block(bj, _):
            bexp_ref[first + bj] = ex
            return 0

        lax.fori_loop(0, nblk, per_block, 0)
        return first + nblk

    nused_ref[0] = lax.fori_loop(0, n_experts, per_expert, 0)


def _tables(dest_flat, meta_flat, t, n_experts, n_blocks, row_block):
    kern = functools.partial(_tables_kernel, t=t, n_experts=n_experts, n_blocks=n_blocks,
                             row_block=row_block)
    smem = pl.BlockSpec(memory_space=pltpu.SMEM)
    return pl.pallas_call(
        kern,
        in_specs=[smem, smem],
        out_specs=[smem, smem, smem],
        out_shape=[jax.ShapeDtypeStruct((n_blocks * row_block,), jnp.int32),
                   jax.ShapeDtypeStruct((n_blocks,), jnp.int32),
                   jax.ShapeDtypeStruct((1,), jnp.int32)],
        name="moe_tables",
    )(dest_flat, meta_flat)


def _expert_kernel(bexp_ref, nused_ref, src_ref, h2_hbm, wg_ref, wu_ref, wd_ref,
                   y_hbm, xbuf, ybuf, wg_bf, wu_bf, wd_bf, gsem, ssem, *, t, rt, row_block, n_blocks):
    i = pl.program_id(0)
    n_used = nused_ref[0]
    xs = i % X_BUFFERS
    ys = i % Y_BUFFERS
    buf_rows = row_block * rt

    def buf(ref, s):
        return ref.at[pl.ds(pl.multiple_of(s * buf_rows, buf_rows), buf_rows), :]

    def gather_row(blk, s, r, prio):
        src = src_ref[blk * row_block + r]
        if t & (t - 1) == 0:
            tok = src & (t - 1)
        else:
            tok = jnp.where(src < 0, 0, jnp.where(src >= t, src - t, src))
        pltpu.make_async_copy(h2_hbm.at[pl.ds(pl.multiple_of(tok * rt, rt), rt), :],
                              xbuf.at[pl.ds(pl.multiple_of((s * row_block + r) * rt, rt), rt), :],
                              gsem.at[s]).start(priority=prio)

    def scatter_row(blk, s, r, prio, to_spare):
        src = src_ref[blk * row_block + r]
        dst = jnp.where((src < 0) | to_spare, 2 * t + s * row_block + r, src)
        pltpu.make_async_copy(ybuf.at[pl.ds(pl.multiple_of((s * row_block + r) * rt, rt), rt), :],
                              y_hbm.at[pl.ds(pl.multiple_of(dst * rt, rt), rt), :],
                              ssem.at[s]).start(priority=prio)

    def for_rows(fn, inline):
        if inline:
            for r in range(row_block):
                fn(r, r % DMA_PRIORITIES)
        else:
            def body(rp, _):
                for prio in range(DMA_PRIORITIES):
                    fn(rp * DMA_PRIORITIES + prio, prio)
                return 0
            lax.fori_loop(0, row_block // DMA_PRIORITIES, body, 0, unroll=4)

    def wait_gather(s):
        pltpu.make_async_copy(buf(xbuf, s), buf(xbuf, s), gsem.at[s]).wait()

    def wait_scatter(s):
        pltpu.make_async_copy(buf(ybuf, s), buf(ybuf, s), ssem.at[s]).wait()

    @pl.when(i == 0)
    def _():
        ybuf[...] = jnp.zeros_like(ybuf)
        for s in range(Y_BUFFERS):
            spare = y_hbm.at[pl.ds((2 * t + s * row_block) * rt, buf_rows), :]
            pltpu.make_async_copy(buf(ybuf, s), spare, ssem.at[s]).start()
        for s in range(Y_BUFFERS):
            wait_scatter(s)
        for_rows(lambda r, p: gather_row(0, 0, r, p), inline=False)

    @pl.when((i == 0) | (bexp_ref[i] != bexp_ref[jnp.maximum(i - 1, 0)]))
    def _():
        wg_bf[...] = wg_ref[...].astype(BF16)
        wu_bf[...] = wu_ref[...].astype(BF16)
        wd_bf[...] = wd_ref[...].astype(BF16)

    prev_ys = (i + Y_BUFFERS - 1) % Y_BUFFERS

    @pl.when(i < n_used)
    def _():
        wait_gather(xs)

        @pl.when(i >= Y_BUFFERS - 1)
        def _():
            wait_scatter(ys)

        x = jnp.concatenate(_load_token_major(xbuf, xs * row_block, row_block, rt), axis=1).astype(BF16)
        for_rows(lambda r, p: gather_row(i + 1, 1 - xs, r, p), inline=True)
        for_rows(lambda r, p: scatter_row(jnp.maximum(i - 1, 0), prev_ys, r, p, i == 0), inline=True)
        gt = jnp.dot(x, wg_bf[...], preferred_element_type=F32)
        up = jnp.dot(x, wu_bf[...], preferred_element_type=F32)
        hid = (jax.nn.silu(gt) * up).astype(BF16)
        _store_token_major(ybuf, ys * row_block, jnp.dot(hid, wd_bf[...], preferred_element_type=F32))

    @pl.when(i == n_used)
    def _():
        wait_gather(xs)
        for_rows(lambda r, p: scatter_row(i - 1, prev_ys, r, p, False), inline=False)

    @pl.when(i == n_blocks - 1)
    def _():
        last = n_used - 1
        wait_scatter(last % Y_BUFFERS)
        wait_scatter((last + Y_BUFFERS - 1) % Y_BUFFERS)

        @pl.when(n_used >= 2)
        def _():
            wait_scatter((last + Y_BUFFERS - 2) % Y_BUFFERS)


def _experts(h2_tm, t, w_gate, w_up, w_down, layer, bexp, nused, src, row_block):
    rt = h2_tm.shape[0] // t
    d = rt * HEAD_DIM
    n_blocks = bexp.shape[0]
    de = w_gate.shape[3]
    kern = functools.partial(_expert_kernel, t=t, rt=rt, row_block=row_block, n_blocks=n_blocks)
    grid_spec = pltpu.PrefetchScalarGridSpec(
        num_scalar_prefetch=3,
        grid=(n_blocks,),
        in_specs=[
            pl.BlockSpec(memory_space=pl.ANY),
            pl.BlockSpec((None, None, d, de), lambda i, be, nu, sr: (layer, be[i], 0, 0)),
            pl.BlockSpec((None, None, d, de), lambda i, be, nu, sr: (layer, be[i], 0, 0)),
            pl.BlockSpec((None, None, de, d), lambda i, be, nu, sr: (layer, be[i], 0, 0)),
        ],
        out_specs=pl.BlockSpec(memory_space=pl.ANY),
        scratch_shapes=[
            pltpu.VMEM((X_BUFFERS * row_block * rt, HEAD_DIM), F32),
            pltpu.VMEM((Y_BUFFERS * row_block * rt, HEAD_DIM), F32),
            pltpu.VMEM((d, de), BF16),
            pltpu.VMEM((d, de), BF16),
            pltpu.VMEM((de, d), BF16),
            pltpu.SemaphoreType.DMA((X_BUFFERS,)),
            pltpu.SemaphoreType.DMA((Y_BUFFERS,)),
        ],
    )
    return pl.pallas_call(
        kern,
        grid_spec=grid_spec,
        out_shape=jax.ShapeDtypeStruct(((2 * t + Y_BUFFERS * row_block) * rt, HEAD_DIM), F32),
        compiler_params=_params(("arbitrary",)),
        name="moe_experts",
    )(bexp, nused, src, h2_tm, w_gate, w_up, w_down)


def _combine_kernel(x_ref, y0_ref, y1_ref, g_ref, o_ref, *, tm, rt):
    g = g_ref[...]
    g0 = jnp.broadcast_to(g[:, 0:1], (tm, HEAD_DIM))
    g1 = jnp.broadcast_to(g[:, 1:2], (tm, HEAD_DIM))
    for j in range(rt):
        sl = slice(j * HEAD_DIM, (j + 1) * HEAD_DIM)
        y0 = y0_ref[pl.ds(j, tm, stride=rt), :]
        y1 = y1_ref[pl.ds(j, tm, stride=rt), :]
        o_ref[:, sl] = x_ref[:, sl] + (y0 * g0 + y1 * g1)


def _combine(x1, y_tok, gates_t):
    t, d = x1.shape
    rt = d // HEAD_DIM
    tm = _tile(t, 512)
    nb = t // tm
    return pl.pallas_call(
        functools.partial(_combine_kernel, tm=tm, rt=rt),
        grid=(nb,),
        in_specs=[pl.BlockSpec((tm, d), lambda i: (i, 0)),
                  pl.BlockSpec((tm * rt, HEAD_DIM), lambda i: (i, 0)),
                  pl.BlockSpec((tm * rt, HEAD_DIM), lambda i: (i + nb, 0)),
                  pl.BlockSpec((tm, 2), lambda i: (i, 0))],
        out_specs=pl.BlockSpec((tm, d), lambda i: (i, 0)),
        out_shape=jax.ShapeDtypeStruct((t, d), F32),
        compiler_params=_params(("arbitrary",)),
        name="moe_combine",
    )(x1, y_tok, y_tok, gates_t)


MOE_ROW_BLOCK = 256


def kernel(x, norm_mix, w_in, conv_a_w, conv_a_b, lru_wa, lru_ba, lru_wx, lru_bx, lru_lam,
           conv_b_w, q_norm, k_norm, mix_norm, w_out, norm_ffn, w_router_group, b_router_group,
           w_router_expert, b_router_expert, w_gate, w_up, w_down):
    bsz, s, d = x.shape
    t = bsz * s
    depth = w_in.shape[0]
    lru_heads = lru_wa.shape[1]
    d_lru = lru_heads * HEAD_DIM
    d_conv = conv_b_w.shape[-1]
    d_sb =(w_in.shape[-1] - 2 * d_lru - 3 * d_conv) // 3
    sb_heads = d_sb // HEAD_DIM
    n_groups = w_router_group.shape[-1]
    n_experts = w_router_expert.shape[-1]
    assert n_experts // n_groups == EXPERTS_PER_GROUP and n_experts + n_groups <= ROUTE_ROWS
    row_block = MOE_ROW_BLOCK
    n_blocks = -(-(2 * t) // row_block) + n_experts

    xt = x.reshape(t, d)
    w_in_bf = w_in.astype(BF16)
    w_out_bf = w_out.astype(BF16)
    for l in range(depth):
        row = lambda v: v[l].reshape(1, -1)
        mix, qkv = _in_projection(xt, row(norm_mix), w_in_bf, l, row(q_norm), row(k_norm), d_sb)
        mix3 = mix.reshape(bsz, s, -1)
        qkv3 = qkv.reshape(bsz, s, -1)
        mg = row(mix_norm)
        w_gates = jnp.concatenate([lru_wa[l], lru_wx[l]], axis=-1).astype(BF16)
        yab = _mixers(mix3, conv_a_w[l], row(conv_a_b), w_gates, row(lru_ba), row(lru_bx),
                      row(lru_lam), conv_b_w[l], mg[:, :d_lru + d_conv])
        yc = _attention(qkv3, mg[:, d_lru + d_conv:], sb_heads)

        pad = jnp.zeros((d, ROUTE_ROWS - n_experts - n_groups), F32)
        w_route_t = jnp.concatenate([w_router_expert[l], w_router_group[l], pad], axis=1).T.astype(BF16)
        b_route = jnp.concatenate([b_router_expert[l], b_router_group[l],
                                   jnp.zeros((ROUTE_ROWS - n_experts - n_groups,), F32)]).reshape(-1, 1)
        x1, h2, logits_t = _out_projection(yab.reshape(t, -1), yc.reshape(t, -1), xt,
                                           w_out_bf, l, row(norm_ffn), w_route_t, b_route)

        dest, gates, meta = _route(logits_t, n_experts, n_groups, row_block)
        src, bexp, nused = _tables(dest.reshape(-1), meta[:, 0], t, n_experts, n_blocks, row_block)
        y_tok = _experts(h2, t, w_gate, w_up, w_down, l, bexp, nused, src, row_block)
        xt = _combine(x1, y_tok, gates.T)
    return xt.reshape(bsz, s, d)
```

```python
import functools
import math

import jax
import jax.numpy as jnp
from jax import lax
from jax.experimental import pallas as pl
from jax.experimental.pallas import tpu as pltpu

EPS = 1e-6
LRU_C = 8.0
LOG2_E = 1.4426950408889634
HEAD_DIM = 128
SUBLANES = 8
EXPERTS_PER_GROUP = 8
ROUTE_ROWS = 48
VMEM_LIMIT = 56 * 1024 * 1024

F32 = jnp.float32
BF16 = jnp.bfloat16


def _tile(n, pref):
    if n <= pref:
        return n
    t = pref
    while n % t:
        t -= SUBLANES
    assert t > 0
    return t


def _params(sem):
    return pltpu.CompilerParams(dimension_semantics=sem, vmem_limit_bytes=VMEM_LIMIT)


def _rms(v):
    return v * lax.rsqrt(jnp.mean(v * v, axis=-1, keepdims=True) + EPS)


def _inproj_kernel(x_ref, g_ref, w_ref, qn_ref, kn_ref, mix_ref, qkv_ref, h_scr, *, n_mix, heads):
    j = pl.program_id(1)

    @pl.when(j == 0)
    def _():
        h_scr[...] = (_rms(x_ref[...]) * g_ref[...]).astype(BF16)

    acc = jnp.dot(h_scr[...], w_ref[...], preferred_element_type=F32)

    @pl.when(j < n_mix)
    def _():
        mix_ref[...] = acc

    def head_norm(gain_ref):
        for h in range(heads):
            sl = slice(h * HEAD_DIM, (h + 1) * HEAD_DIM)
            qkv_ref[:, sl] = (_rms(acc[:, sl]) * gain_ref[...]).astype(BF16)

    @pl.when(j == n_mix)
    def _():
        head_norm(qn_ref)

    @pl.when(j == n_mix + 1)
    def _():
        head_norm(kn_ref)

    @pl.when(j == n_mix + 2)
    def _():
        qkv_ref[...] = acc.astype(BF16)


def _in_projection(x, gain, w_bf16, layer, q_gain, k_gain, d_sb):
    t, d = x.shape
    d_in = w_bf16.shape[2]
    tn = d_sb
    assert d_in % tn == 0 and (d_in - 3 * d_sb) % tn == 0
    n_mix = (d_in - 3 * d_sb) // tn
    tm = _tile(t, 1024)
    kern = functools.partial(_inproj_kernel, n_mix=n_mix, heads=d_sb // HEAD_DIM)
    return pl.pallas_call(
        kern,
        grid=(t // tm, d_in // tn),
        in_specs=[
            pl.BlockSpec((tm, d), lambda i, j: (i, 0)),
            pl.BlockSpec((1, d), lambda i, j: (0, 0)),
            pl.BlockSpec((None, d, tn), lambda i, j: (layer, 0, j)),
            pl.BlockSpec((1, HEAD_DIM), lambda i, j: (0, 0)),
            pl.BlockSpec((1, HEAD_DIM), lambda i, j: (0, 0)),
        ],
        out_specs=[
            pl.BlockSpec((tm, tn), lambda i, j: (i, jnp.minimum(j, n_mix - 1))),
            pl.BlockSpec((tm, tn), lambda i, j: (i, jnp.maximum(j - n_mix, 0))),
        ],
        out_shape=[
            jax.ShapeDtypeStruct((t, n_mix * tn), F32),
            jax.ShapeDtypeStruct((t, 3 * d_sb), BF16),
        ],
        scratch_shapes=[pltpu.VMEM((tm, d), BF16)],
        compiler_params=_params(("arbitrary", "arbitrary")),
        name="in_projection",
    )(x, gain, w_bf16, q_gain, k_gain)


def _softplus(v):
    return jnp.maximum(v, 0.0) + jnp.log1p(jnp.exp(-jnp.abs(v)))


def _sigmoid(v):
    return 0.5 * (jnp.tanh(0.5 * v) + 1.0)


def _causal_conv(ext_scr, cols, new_rows, cw, ts):
    taps = cw.shape[0]
    ext_scr[SUBLANES:SUBLANES + ts, cols] = new_rows
    out = None
    for k in range(taps):
        lo = SUBLANES - (taps - 1) + k
        term = cw[k:k + 1, :] * ext_scr[lo:lo + ts, cols]
        out = term if out is None else out + term
    ext_scr[0:SUBLANES, cols] = ext_scr[ts:ts + SUBLANES, cols]
    return out


def _mixers_kernel(mix_ref, cwa_ref, cba_ref, wg_ref, ba_ref, bx_ref, lam_ref, cwb_ref, mn_ref,
                   out_ref, exta_scr, extb_scr, a_scr, u_scr, hc_scr, *, ts, heads, groups):
    si = pl.program_id(1)
    d_lru = heads * HEAD_DIM
    d_conv = groups * HEAD_DIM

    @pl.when(si == 0)
    def _():
        exta_scr[0:SUBLANES, :] = jnp.zeros((SUBLANES, d_lru), F32)
        extb_scr[0:SUBLANES, :] = jnp.zeros((SUBLANES, d_conv), F32)
        hc_scr[...] = jnp.zeros_like(hc_scr)

    sub = lax.broadcasted_iota(jnp.int32, (ts, HEAD_DIM), 0) & (SUBLANES - 1)

    for h in range(heads):
        cols = slice(h * HEAD_DIM, (h + 1) * HEAD_DIM)
        xc = _causal_conv(exta_scr, cols, mix_ref[0, :, cols], cwa_ref[:, cols], ts) + cba_ref[:, cols]
        gates = jnp.dot(xc.astype(BF16), wg_ref[h], preferred_element_type=F32)
        r = _sigmoid(gates[:, :HEAD_DIM] + ba_ref[:, cols])
        i = _sigmoid(gates[:, HEAD_DIM:] + bx_ref[:, cols])
        log_a = (-LRU_C) * r * _softplus(-lam_ref[:, cols])
        a = jnp.exp(log_a)
        u = jnp.sqrt(-jnp.tanh(log_a) * (a * a + 1.0)) * (i * xc)

        d = 1
        while d < SUBLANES:
            keep = sub >= d
            a_sh = jnp.where(keep, pltpu.roll(a, d, 0), 1.0)
            u_sh = jnp.where(keep, pltpu.roll(u, d, 0), 0.0)
            u = a * u_sh + u
            a = a * a_sh
            d *= 2
        a_scr[...] = a
        u_scr[...] = u

        def carry_step(c, hb):
            off = pl.multiple_of(c * SUBLANES, SUBLANES)
            hrow = u_scr[pl.ds(off, SUBLANES), :] + a_scr[pl.ds(off, SUBLANES), :] * hb
            u_scr[pl.ds(off, SUBLANES), :] = hrow
            return jnp.broadcast_to(hrow[SUBLANES - 1:SUBLANES, :], (SUBLANES, HEAD_DIM))

        hc_scr[:, cols] = lax.fori_loop(0, ts // SUBLANES, carry_step, hc_scr[:, cols], unroll=8)

        y = u_scr[...] * jax.nn.gelu(mix_ref[0, :, d_lru + h * HEAD_DIM:d_lru + (h + 1) * HEAD_DIM])
        out_ref[0, :, cols] = (_rms(y) * mn_ref[:, cols]).astype(BF16)

    for g in range(groups):
        cols = slice(g * HEAD_DIM, (g + 1) * HEAD_DIM)
        col = lambda part: slice(2 * d_lru + part * d_conv + g * HEAD_DIM,
                                 2 * d_lru + part * d_conv + (g + 1) * HEAD_DIM)
        conv = _causal_conv(extb_scr, cols, mix_ref[0, :, col(1)] * mix_ref[0, :, col(2)], cwb_ref[:, cols], ts)
        y = mix_ref[0, :, col(0)] * conv
        ocols = slice(d_lru + g * HEAD_DIM, d_lru + (g + 1) * HEAD_DIM)
        out_ref[0, :, ocols] = (_rms(y) * mn_ref[:, ocols]).astype(BF16)


def _mixers(mix3, conv_a_w, conv_a_b, w_gates, b_a, b_x, lam, conv_b_w, mix_gain):
    b, s, d_mix = mix3.shape
    heads = w_gates.shape[0]
    d_lru = heads * HEAD_DIM
    d_conv = conv_b_w.shape[1]
    groups = d_conv // HEAD_DIM
    assert d_mix == 2 * d_lru + 3 * d_conv
    ts = _tile(s, 512)
    assert max(conv_a_w.shape[0], conv_b_w.shape[0]) - 1 <= SUBLANES and ts % SUBLANES == 0
    kern = functools.partial(_mixers_kernel, ts=ts, heads=heads, groups=groups)
    full = lambda a: pl.BlockSpec(a.shape, lambda bi, si: (0,) * a.ndim)
    return pl.pallas_call(
        kern,
        grid=(b, s // ts),
        in_specs=[pl.BlockSpec((1, ts, d_mix), lambda bi, si: (bi, si, 0)),
                  full(conv_a_w), full(conv_a_b), full(w_gates), full(b_a), full(b_x), full(lam),
                  full(conv_b_w), full(mix_gain)],
        out_specs=pl.BlockSpec((1, ts, d_lru + d_conv), lambda bi, si: (bi, si, 0)),
        out_shape=jax.ShapeDtypeStruct((b, s, d_lru + d_conv), BF16),
        scratch_shapes=[
            pltpu.VMEM((ts + SUBLANES, d_lru), F32),
            pltpu.VMEM((ts + SUBLANES, d_conv), F32),
            pltpu.VMEM((ts, HEAD_DIM), F32),
            pltpu.VMEM((ts, HEAD_DIM), F32),
            pltpu.VMEM((SUBLANES, d_lru), F32),
        ],
        compiler_params=_params(("arbitrary", "arbitrary")),
        name="mixers_rglru_conv",
    )(mix3, conv_a_w, conv_a_b, w_gates, b_a, b_x, lam, conv_b_w, mix_gain)


def _attn_kernel(q_ref, k_ref, v_ref, m_ref, mn_ref, o_ref, acc_scr, run_scr, *, bq, bk, scale):
    qi = pl.program_id(2)
    q = q_ref[0]
    n_sub = bq // bk

    def log_sigmoids(s):
        z = s * scale
        ls = jnp.minimum(z, 0.0) - jnp.log(1.0 + jnp.exp2(jnp.abs(s) * (-scale * LOG2_E)))
        return ls, ls - z

    def suffix_sums(lk):
        hi = lk.astype(BF16)
        lo = (lk - hi.astype(F32)).astype(BF16)
        return jnp.dot(jnp.concatenate([hi, lo], axis=1), m_ref[...], preferred_element_type=F32)

    def chunk(c0, run):
        kc = k_ref[0, pl.ds(c0, bq), :]
        vc = v_ref[0, pl.ds(c0, bq), :]
        ls, lk = log_sigmoids(lax.dot_general(q, kc, (((1,), (1,)), ((), ())), preferred_element_type=F32))
        ws = [None] * n_sub
        for k in reversed(range(n_sub)):
            sl = slice(k * bk, (k + 1) * bk)
            sums = suffix_sums(lk[:, sl])
            ws[k] = jnp.exp(ls[:, sl] + sums[:, :bk] + run).astype(BF16)
            run = run + sums[:, bk:]
        pv = jnp.dot(jnp.concatenate(ws, axis=1), vc, preferred_element_type=F32)
        return pv, run

    def diagonal_chunk(c0):
        kc = k_ref[0, pl.ds(c0, bq), :]
        vc = v_ref[0, pl.ds(c0, bq), :]
        s = lax.dot_general(q, kc, (((1,), (1,)), ((), ())), preferred_element_type=F32)
        run = jnp.zeros((bq, HEAD_DIM), F32)
        ws = [None] * n_sub
        for k in reversed(range(n_sub)):
            r0 = k * bk
            rows = bq - r0
            ls, lk = log_sigmoids(s[r0:, r0:r0 + bk])
            mask = (lax.broadcasted_iota(jnp.int32, (rows, bk), 1)
                    < lax.broadcasted_iota(jnp.int32, (rows, bk), 0))
            sums = suffix_sums(jnp.where(mask, lk, 0.0))
            w = jnp.where(mask, jnp.exp(ls + sums[:, :bk] + run[r0:]), 0.0).astype(BF16)
            seen = run[r0:] + sums[:, bk:]
            if r0:
                w = jnp.concatenate([jnp.zeros((r0, bk), BF16), w], axis=0)
                seen = jnp.concatenate([run[:r0], seen], axis=0)
            ws[k] = w
            run = seen
        pv = jnp.dot(jnp.concatenate(ws, axis=1), vc, preferred_element_type=F32)
        return pv, run

    acc, run = diagonal_chunk(pl.multiple_of(qi * bq, bq))
    acc_scr[...] = acc
    run_scr[...] = run

    def chunk_pair(p, _):
        pv_a, run_a = chunk(pl.multiple_of((qi - 1 - 2 * p) * bq, bq), run_scr[...])
        pv_b, run_b = chunk(pl.multiple_of((qi - 2 - 2 * p) * bq, bq), run_a)
        acc_scr[...] += pv_a + pv_b
        run_scr[...] = run_b
        return 0

    lax.fori_loop(0, qi // 2, chunk_pair, 0)

    @pl.when(qi % 2 == 1)
    def _():
        pv, _ = chunk(0, run_scr[...])
        acc_scr[...] += pv

    o_ref[0] = (_rms(acc_scr[...]) * mn_ref[...]).astype(BF16)


def _suffix_matrix(bk):
    j = jnp.arange(bk)[:, None]
    s = jnp.arange(bk)[None, :]
    half = jnp.concatenate([(j > s).astype(BF16), jnp.ones((bk, HEAD_DIM), BF16)], axis=1)
    return jnp.concatenate([half, half], axis=0)


def _attention(qkv3, mix_gain, heads):
    b, s, _ = qkv3.shape
    bk = HEAD_DIM
    bq = _tile(s, 512)
    assert bq % bk == 0 and s % bq == 0
    kern = functools.partial(_attn_kernel, bq=bq, bk=bk, scale=1.0 / math.sqrt(HEAD_DIM))
    return pl.pallas_call(
        kern,
        grid=(b, heads, s // bq),
        in_specs=[
            pl.BlockSpec((1, bq, HEAD_DIM), lambda bi, h, qi: (bi, qi, h)),
            pl.BlockSpec((1, s, HEAD_DIM), lambda bi, h, qi: (bi, 0, heads + h)),
            pl.BlockSpec((1, s, HEAD_DIM), lambda bi, h, qi: (bi, 0, 2 * heads + h)),
            pl.BlockSpec((2 * bk, bk + HEAD_DIM), lambda bi, h, qi: (0, 0)),
            pl.BlockSpec((1, HEAD_DIM), lambda bi, h, qi: (0, h)),
        ],
        out_specs=pl.BlockSpec((1, bq, HEAD_DIM), lambda bi, h, qi: (bi, qi, h)),
        out_shape=jax.ShapeDtypeStruct((b, s, heads * HEAD_DIM), BF16),
        scratch_shapes=[pltpu.VMEM((bq, HEAD_DIM), F32), pltpu.VMEM((bq, HEAD_DIM), F32)],
        compiler_params=_params(("arbitrary", "arbitrary", "arbitrary")),
        name="stickbreak_attention",
    )(qkv3, qkv3, qkv3, _suffix_matrix(bk), mix_gain)


def _outproj_kernel(yab_ref, yc_ref, x_ref, w_ref, g_ref, wr_ref, br_ref,
                    x1_ref, h2_ref, lg_ref, *, dab, tm, parts):
    rows = tm // parts
    for p in range(parts):
        rs = slice(p * rows, (p + 1) * rows)
        acc = jnp.dot(yab_ref[rs, :], w_ref[0:dab, :], preferred_element_type=F32)
        acc = acc + jnp.dot(yc_ref[rs, :], w_ref[dab:, :], preferred_element_type=F32)
        x1 = x_ref[rs, :] + acc
        x1_ref[rs, :] = x1
        h2 = (_rms(x1) * g_ref[...]).astype(BF16)
        half = h2.shape[1] // 2
        lo = lax.bitcast_convert_type(h2[:, :half].astype(F32), jnp.int32)
        hi = lax.bitcast_convert_type(h2[:, half:].astype(F32), jnp.int32)
        _store_token_major(h2_ref, p * rows, lax.shift_right_logical(lo, 16) | hi)
        lg = lax.dot_general(wr_ref[...], h2, (((1,), (1,)), ((), ())),
                             preferred_element_type=F32)
        lg_ref[:, rs] = lg + br_ref[...]


def _store_token_major(ref, first_row, val):
    rows, d = val.shape
    rt = d // HEAD_DIM
    for j in range(rt):
        ref[pl.ds(first_row * rt + j, rows, stride=rt), :] = val[:, j * HEAD_DIM:(j + 1) * HEAD_DIM]


def _load_token_major(ref, first_row, rows, rt):
    return [ref[pl.ds(first_row * rt + j, rows, stride=rt), :] for j in range(rt)]


def _out_projection(yab, yc, x, w_bf16, layer, gain, w_route_t, b_route):
    t, d = x.shape
    dab, dc = yab.shape[1], yc.shape[1]
    rt = d // HEAD_DIM
    tm = _tile(t, 512)
    parts = 2 if tm % (2 * HEAD_DIM) == 0 else 1
    kern = functools.partial(_outproj_kernel, dab=dab, tm=tm, parts=parts)
    row = lambda n: pl.BlockSpec((tm, n), lambda i: (i, 0))
    full = lambda shp: pl.BlockSpec(shp, lambda i: (0, 0))
    return pl.pallas_call(
        kern,
        grid=(t // tm,),
        in_specs=[row(dab), row(dc), row(d),
                  pl.BlockSpec((None, dab + dc, d), lambda i: (layer, 0, 0),
                               pipeline_mode=pl.Buffered(1)),
                  full((1, d)),
                  full((ROUTE_ROWS, d)), full((ROUTE_ROWS, 1))],
        out_specs=[row(d), pl.BlockSpec((tm * rt // 2, HEAD_DIM), lambda i: (i, 0)),
                   pl.BlockSpec((ROUTE_ROWS, tm), lambda i: (0, i))],
        out_shape=[jax.ShapeDtypeStruct((t, d), F32), jax.ShapeDtypeStruct((t * rt // 2, HEAD_DIM), jnp.int32),
                   jax.ShapeDtypeStruct((ROUTE_ROWS, t), F32)],
        compiler_params=_params(("arbitrary",)),
        name="out_projection",
    )(yab, yc, x, w_bf16, gain, w_route_t, b_route)


def _route_kernel(lg_ref, tri_ref, ones_ref, ltri_ref, dest_ref, gate_ref, meta_ref,
                  cnt_scr, pst_scr, run_scr, *, n_experts, n_groups, row_block, tl):
    ph = pl.program_id(0)
    i = pl.program_id(1)
    epg = n_experts // n_groups

    @pl.when((ph == 0) & (i == 0))
    def _():
        cnt_scr[...] = jnp.zeros_like(cnt_scr)

    lg = lg_ref[...]
    grp = lg[n_experts:n_experts + n_groups, :]
    gmax = jnp.max(grp, axis=0, keepdims=True)
    giota = lax.broadcasted_iota(jnp.int32, grp.shape, 0)
    gidx = jnp.min(jnp.where(grp == gmax, giota, n_groups), axis=0, keepdims=True)
    p_grp = 1.0 / jnp.sum(jnp.exp(grp - gmax), axis=0, keepdims=True)

    sel = lg[0:epg, :]
    for g in range(1, n_groups):
        sel = jnp.where(gidx == g, lg[g * epg:(g + 1) * epg, :], sel)
    eiota = lax.broadcasted_iota(jnp.int32, sel.shape, 0)
    m1 = jnp.max(sel, axis=0, keepdims=True)
    i1 = jnp.min(jnp.where(sel == m1, eiota, epg), axis=0, keepdims=True)
    rest = jnp.where(eiota == i1, -jnp.inf, sel)
    m2 = jnp.max(rest, axis=0, keepdims=True)
    i2 = jnp.min(jnp.where(rest == m2, eiota, epg), axis=0, keepdims=True)
    e = jnp.exp(m2 - m1)
    den = 1.0 + e
    ex1 = gidx * epg + i1
    ex2 = gidx * epg + i2

    xiota = lax.broadcasted_iota(jnp.int32, (n_experts, tl), 0)
    hit1 = xiota == ex1
    hit2 = xiota == ex2
    onehot = (hit1 | hit2).astype(BF16)
    tile_cnt = jnp.dot(onehot, ones_ref[...], preferred_element_type=F32)

    @pl.when(ph == 0)
    def _():
        cnt_scr[...] += tile_cnt

    @pl.when((ph == 1) & (i == 0))
    def _():
        cnt = cnt_scr[...].astype(jnp.int32)
        nblk = (cnt + (row_block - 1)) // row_block
        pst = jnp.dot(ltri_ref[...], nblk.astype(F32).astype(BF16), preferred_element_type=F32)
        pst_scr[...] = pst
        run_scr[...] = jnp.zeros_like(run_scr)
        meta_ref[0:n_experts, :] = pst.astype(jnp.int32)
        meta_ref[n_experts:2 * n_experts, :] = cnt

    @pl.when(ph == 1)
    def _():
        rank = jnp.dot(onehot, tri_ref[...], preferred_element_type=F32)
        base = pst_scr[...] * float(row_block) + run_scr[...]
        slot = jnp.concatenate([base] * (tl // HEAD_DIM), axis=1) + rank
        d1 = jnp.sum(jnp.where(hit1, slot, 0.0), axis=0, keepdims=True)
        d2 = jnp.sum(jnp.where(hit2, slot, 0.0), axis=0, keepdims=True)
        dest_ref[0:1, :] = d1.astype(jnp.int32)
        dest_ref[1:2, :] = d2.astype(jnp.int32)
        gate_ref[0:1, :] = p_grp * (1.0 / den)
        gate_ref[1:2, :] = p_grp * (e / den)
        run_scr[...] += tile_cnt


def _route(logits_t, n_experts, n_groups, row_block):
    t = logits_t.shape[1]
    tl = _tile(t, 512)
    assert tl % HEAD_DIM == 0
    a = jnp.arange(tl)
    tri = (a[:, None] < a[None, :]).astype(BF16)
    ones = jnp.ones((tl, HEAD_DIM), BF16)
    ea = jnp.arange(n_experts)
    ltri = (ea[None, :] < ea[:, None]).astype(BF16)
    kern = functools.partial(_route_kernel, n_experts=n_experts, n_groups=n_groups,
                             row_block=row_block, tl=tl)
    full = lambda shp: pl.BlockSpec(shp, lambda p, i: (0, 0))
    return pl.pallas_call(
        kern,
        grid=(2, t // tl),
        in_specs=[pl.BlockSpec((ROUTE_ROWS, tl), lambda p, i: (0, i)),
                  full((tl, tl)), full((tl, HEAD_DIM)), full((n_experts, n_experts))],
        out_specs=[pl.BlockSpec((2, tl), lambda p, i: (0, i * p)),
                   pl.BlockSpec((2, tl), lambda p, i: (0, i * p)),
                   full((2 * n_experts, HEAD_DIM))],
        out_shape=[jax.ShapeDtypeStruct((2, t), jnp.int32), jax.ShapeDtypeStruct((2, t), F32),
                   jax.ShapeDtypeStruct((2 * n_experts, HEAD_DIM), jnp.int32)],
        scratch_shapes=[pltpu.VMEM((n_experts, HEAD_DIM), F32),
                        pltpu.VMEM((n_experts, HEAD_DIM), F32),
                        pltpu.VMEM((n_experts, HEAD_DIM), F32)],
        compiler_params=_params(("arbitrary", "arbitrary")),
        name="moe_route",
    )(logits_t, tri, ones, ltri)


PAD_ROW = -1
DMA_PRIORITIES = 2
X_BUFFERS = 3
Y_BUFFERS = 3
HIGH_HALF = -65536


def _tables_kernel(dest_ref, meta_ref, src_ref, bexp_ref, nused_ref, *,
                   t, n_experts, n_blocks, row_block):
    def mark_pad(ri, _):
        src_ref[ri] = PAD_ROW
        return 0

    lax.fori_loop(0, n_blocks * row_block, mark_pad, 0, unroll=16)

    def fill(tok, _):
        src_ref[dest_ref[tok]] = tok
        src_ref[dest_ref[t + tok]] = t + tok
        return 0

    lax.fori_loop(0, t, fill, 0, unroll=8)

    def clear(bi, _):
        bexp_ref[bi] = n_experts - 1
        return 0

    lax.fori_loop(0, n_blocks, clear, 0)

    def per_expert(ex, used):
        first = meta_ref[ex]
        nblk = (meta_ref[n_experts + ex] + (row_block - 1)) // row_block

        def per_block(bj, _):
            bexp_ref[first + bj] = ex
            return 0

        lax.fori_loop(0, nblk, per_block, 0)
        return first + nblk

    nused_ref[0] = lax.fori_loop(0, n_experts, per_expert, 0)


def _tables(dest_flat, meta_flat, t, n_experts, n_blocks, row_block):
    kern = functools.partial(_tables_kernel, t=t, n_experts=n_experts, n_blocks=n_blocks,
                             row_block=row_block)
    smem = pl.BlockSpec(memory_space=pltpu.SMEM)
    return pl.pallas_call(
        kern,
        in_specs=[smem, smem],
        out_specs=[smem, smem, smem],
        out_shape=[jax.ShapeDtypeStruct((n_blocks * row_block,), jnp.int32),
                   jax.ShapeDtypeStruct((n_blocks,), jnp.int32),
                   jax.ShapeDtypeStruct((1,), jnp.int32)],
        name="moe_tables",
    )(dest_flat, meta_flat)


def _expert_kernel(bexp_ref, nused_ref, src_ref, h2_hbm, wg_ref, wu_ref, wd_ref,
                   y_hbm, xbuf, ybuf, wg_bf, wu_bf, wd_bf, gsem, ssem, *, t, rt, row_block, n_blocks):
    i = pl.program_id(0)
    n_used = nused_ref[0]
    xs = i % X_BUFFERS
    ys = i % Y_BUFFERS
    xt = rt // 2

    def buf(ref, s, tiles):
        n = row_block * tiles
        return ref.at[pl.ds(pl.multiple_of(s * n, n), n), :]

    def for_rows(fn):
        def body(rp, _):
            for prio in range(DMA_PRIORITIES):
                fn(rp * DMA_PRIORITIES + prio, prio)
            return 0
        lax.fori_loop(0, row_block // DMA_PRIORITIES, body, 0, unroll=4)

    def start_gather(blk):
        s = blk % X_BUFFERS

        def row(r, prio):
            src = src_ref[blk * row_block + r]
            if t & (t - 1) == 0:
                tok = src & (t - 1)
            else:
                tok = jnp.where(src < 0, 0, jnp.where(src >= t, src - t, src))
            pltpu.make_async_copy(h2_hbm.at[pl.ds(pl.multiple_of(tok * xt, xt), xt), :],
                                  xbuf.at[pl.ds(pl.multiple_of((s * row_block + r) * xt, xt), xt), :],
                                  gsem.at[s]).start(priority=prio)
        for_rows(row)

    def start_scatter(blk):
        s = blk % Y_BUFFERS

        def row(r, prio):
            src = src_ref[blk * row_block + r]
            dst = jnp.where(src < 0, 2 * t + s * row_block + r, src)
            pltpu.make_async_copy(ybuf.at[pl.ds(pl.multiple_of((s * row_block + r) * rt, rt), rt), :],
                                  y_hbm.at[pl.ds(pl.multiple_of(dst * rt, rt), rt), :],
                                  ssem.at[s]).start(priority=prio)
        for_rows(row)

    def wait_gather(s):
        pltpu.make_async_copy(buf(xbuf, s, xt), buf(xbuf, s, xt), gsem.at[s]).wait()

    def wait_scatter(s):
        pltpu.make_async_copy(buf(ybuf, s, rt), buf(ybuf, s, rt), ssem.at[s]).wait()

    @pl.when(i == 0)
    def _():
        ybuf[...] = jnp.zeros_like(ybuf)
        for s in range(Y_BUFFERS):
            spare = y_hbm.at[pl.ds((2 * t + s * row_block) * rt, row_block * rt), :]
            pltpu.make_async_copy(buf(ybuf, s, rt), spare, ssem.at[s]).start()
        for s in range(Y_BUFFERS):
            wait_scatter(s)
        for first in range(X_BUFFERS - 1):
            @pl.when(first < n_used)
            def _():
                start_gather(first)

    @pl.when((i == 0) | (bexp_ref[i] != bexp_ref[jnp.maximum(i - 1, 0)]))
    def _():
        wg_bf[...] = wg_ref[...].astype(BF16)
        wu_bf[...] = wu_ref[...].astype(BF16)
        wd_bf[...] = wd_ref[...].astype(BF16)

    @pl.when(i + (X_BUFFERS - 1) < n_used)
    def _():
        start_gather(i + (X_BUFFERS - 1))

    @pl.when(i < n_used)
    def _():
        wait_gather(xs)

        @pl.when(i >= Y_BUFFERS)
        def _():
            wait_scatter(ys)

        words = _load_token_major(xbuf, xs * row_block, row_block, xt)
        lo = [lax.bitcast_convert_type(w << 16, F32).astype(BF16) for w in words]
        hi = [lax.bitcast_convert_type(w & HIGH_HALF, F32).astype(BF16) for w in words]
        x = jnp.concatenate(lo + hi, axis=1)
        gt = jnp.dot(x, wg_bf[...], preferred_element_type=F32)
        up = jnp.dot(x, wu_bf[...], preferred_element_type=F32)
        hid = (jax.nn.silu(gt) * up).astype(BF16)
        _store_token_major(ybuf, ys * row_block, jnp.dot(hid, wd_bf[...], preferred_element_type=F32))
        start_scatter(i)

    @pl.when(i == n_blocks - 1)
    def _():
        for back in range(Y_BUFFERS):
            @pl.when(n_used > back)
            def _():
                wait_scatter((n_used - 1 - back) % Y_BUFFERS)


def _experts(h2_tm, t, w_gate, w_up, w_down, layer, bexp, nused, src, row_block):
    d = w_gate.shape[2]
    rt = d // HEAD_DIM
    assert h2_tm.shape == (t * rt // 2, HEAD_DIM) and h2_tm.dtype == jnp.int32
    n_blocks = bexp.shape[0]
    de = w_gate.shape[3]
    kern = functools.partial(_expert_kernel, t=t, rt=rt, row_block=row_block, n_blocks=n_blocks)
    grid_spec = pltpu.PrefetchScalarGridSpec(
        num_scalar_prefetch=3,
        grid=(n_blocks,),
        in_specs=[
            pl.BlockSpec(memory_space=pl.ANY),
            pl.BlockSpec((None, None, d, de), lambda i, be, nu, sr: (layer, be[i], 0, 0)),
            pl.BlockSpec((None, None, d, de), lambda i, be, nu, sr: (layer, be[i], 0, 0)),
            pl.BlockSpec((None, None, de, d), lambda i, be, nu, sr: (layer, be[i], 0, 0)),
        ],
        out_specs=pl.BlockSpec(memory_space=pl.ANY),
        scratch_shapes=[
            pltpu.VMEM((X_BUFFERS * row_block * rt // 2, HEAD_DIM), jnp.int32),
            pltpu.VMEM((Y_BUFFERS * row_block * rt, HEAD_DIM), F32),
            pltpu.VMEM((d, de), BF16),
            pltpu.VMEM((d, de), BF16),
            pltpu.VMEM((de, d), BF16),
            pltpu.SemaphoreType.DMA((X_BUFFERS,)),
            pltpu.SemaphoreType.DMA((Y_BUFFERS,)),
        ],
    )
    return pl.pallas_call(
        kern,
        grid_spec=grid_spec,
        out_shape=jax.ShapeDtypeStruct(((2 * t + Y_BUFFERS * row_block) * rt, HEAD_DIM), F32),
        compiler_params=_params(("arbitrary",)),
        name="moe_experts",
    )(bexp, nused, src, h2_tm, w_gate, w_up, w_down)


def _combine_kernel(x_ref, y0_ref, y1_ref, g_ref, o_ref, *, tm, rt):
    g = g_ref[...]
    g0 = jnp.broadcast_to(g[:, 0:1], (tm, HEAD_DIM))
    g1 = jnp.broadcast_to(g[:, 1:2], (tm, HEAD_DIM))
    for j in range(rt):
        sl = slice(j * HEAD_DIM, (j + 1) * HEAD_DIM)
        y0 = y0_ref[pl.ds(j, tm, stride=rt), :]
        y1 = y1_ref[pl.ds(j, tm, stride=rt), :]
        o_ref[:, sl] = x_ref[:, sl] + (y0 * g0 + y1 * g1)


def _combine(x1, y_tok, gates_t):
    t, d = x1.shape
    rt = d // HEAD_DIM
    tm = _tile(t, 512)
    nb = t // tm
    return pl.pallas_call(
        functools.partial(_combine_kernel, tm=tm, rt=rt),
        grid=(nb,),
        in_specs=[pl.BlockSpec((tm, d), lambda i: (i, 0)),
                  pl.BlockSpec((tm * rt, HEAD_DIM), lambda i: (i, 0)),
                  pl.BlockSpec((tm * rt, HEAD_DIM), lambda i: (i + nb, 0)),
                  pl.BlockSpec((tm, 2), lambda i: (i, 0))],
        out_specs=pl.BlockSpec((tm, d), lambda i: (i, 0)),
        out_shape=jax.ShapeDtypeStruct((t, d), F32),
        compiler_params=_params(("arbitrary",)),
        name="moe_combine",
    )(x1, y_tok, y_tok, gates_t)


MOE_ROW_BLOCK = 256


def kernel(x, norm_mix, w_in, conv_a_w, conv_a_b, lru_wa, lru_ba, lru_wx, lru_bx, lru_lam,
           conv_b_w, q_norm, k_norm, mix_norm, w_out, norm_ffn, w_router_group, b_router_group,
           w_router_expert, b_router_expert, w_gate, w_up, w_down):
    bsz, s, d = x.shape
    t = bsz * s
    depth = w_in.shape[0]
    lru_heads = lru_wa.shape[1]
    d_lru = lru_heads * HEAD_DIM
    d_conv = conv_b_w.shape[-1]
    d_sb =(w_in.shape[-1] - 2 * d_lru - 3 * d_conv) // 3
    sb_heads = d_sb // HEAD_DIM
    n_groups = w_router_group.shape[-1]
    n_experts = w_router_expert.shape[-1]
    assert n_experts // n_groups == EXPERTS_PER_GROUP and n_experts + n_groups <= ROUTE_ROWS
    row_block = MOE_ROW_BLOCK
    n_blocks = -(-(2 * t) // row_block) + n_experts

    xt = x.reshape(t, d)
    w_in_bf = w_in.astype(BF16)
    w_out_bf = w_out.astype(BF16)
    for l in range(depth):
        row = lambda v: v[l].reshape(1, -1)
        mix, qkv = _in_projection(xt, row(norm_mix), w_in_bf, l, row(q_norm), row(k_norm), d_sb)
        mix3 = mix.reshape(bsz, s, -1)
        qkv3 = qkv.reshape(bsz, s, -1)
        mg = row(mix_norm)
        w_gates = jnp.concatenate([lru_wa[l], lru_wx[l]], axis=-1).astype(BF16)
        yab = _mixers(mix3, conv_a_w[l], row(conv_a_b), w_gates, row(lru_ba), row(lru_bx),
                      row(lru_lam), conv_b_w[l], mg[:, :d_lru + d_conv])
        yc = _attention(qkv3, mg[:, d_lru + d_conv:], sb_heads)

        pad = jnp.zeros((d, ROUTE_ROWS - n_experts - n_groups), F32)
        w_route_t = jnp.concatenate([w_router_expert[l], w_router_group[l], pad], axis=1).T.astype(BF16)
        b_route = jnp.concatenate([b_router_expert[l], b_router_group[l],
                                   jnp.zeros((ROUTE_ROWS - n_experts - n_groups,), F32)]).reshape(-1, 1)
        x1, h2, logits_t = _out_projection(yab.reshape(t, -1), yc.reshape(t, -1), xt,
                                           w_out_bf, l, row(norm_ffn), w_route_t, b_route)

        dest, gates, meta = _route(logits_t, n_experts, n_groups, row_block)
        src, bexp, nused = _tables(dest.reshape(-1), meta[:, 0], t, n_experts, n_blocks, row_block)
        y_tok = _experts(h2, t, w_gate, w_up, w_down, l, bexp, nused, src, row_block)
        xt = _combine(x1, y_tok, gates.T)
    return xt.reshape(bsz, s, d)
```

```python
import functools
import math

import jax
import jax.numpy as jnp
from jax import lax
from jax.experimental import pallas as pl
from jax.experimental.pallas import tpu as pltpu

EPS = 1e-6
LRU_C = 8.0
LOG2_E = 1.4426950408889634
HEAD_DIM = 128
SUBLANES = 8
EXPERTS_PER_GROUP = 8
ROUTE_ROWS = 48
VMEM_LIMIT = 56 * 1024 * 1024

F32 = jnp.float32
BF16 = jnp.bfloat16


def _tile(n, pref):
    if n <= pref:
        return n
    t = pref
    while n % t:
        t -= SUBLANES
    assert t > 0
    return t


def _params(sem):
    return pltpu.CompilerParams(dimension_semantics=sem, vmem_limit_bytes=VMEM_LIMIT)


def _rms(v):
    return v * lax.rsqrt(jnp.mean(v * v, axis=-1, keepdims=True) + EPS)


def _inproj_kernel(x_ref, g_ref, w_ref, qn_ref, kn_ref, mix_ref, qkv_ref, h_scr, *, n_mix, heads):
    j = pl.program_id(1)

    @pl.when(j == 0)
    def _():
        h_scr[...] = (_rms(x_ref[...]) * g_ref[...]).astype(BF16)

    acc = jnp.dot(h_scr[...], w_ref[...], preferred_element_type=F32)

    @pl.when(j < n_mix)
    def _():
        mix_ref[...] = acc

    def head_norm(gain_ref):
        for h in range(heads):
            sl = slice(h * HEAD_DIM, (h + 1) * HEAD_DIM)
            qkv_ref[:, sl] = (_rms(acc[:, sl]) * gain_ref[...]).astype(BF16)

    @pl.when(j == n_mix)
    def _():
        head_norm(qn_ref)

    @pl.when(j == n_mix + 1)
    def _():
        head_norm(kn_ref)

    @pl.when(j == n_mix + 2)
    def _():
        qkv_ref[...] = acc.astype(BF16)


def _in_projection(x, gain, w_bf16, layer, q_gain, k_gain, d_sb):
    t, d = x.shape
    d_in = w_bf16.shape[2]
    tn = d_sb
    assert d_in % tn == 0 and (d_in - 3 * d_sb) % tn == 0
    n_mix = (d_in - 3 * d_sb) // tn
    tm = _tile(t, 1024)
    kern = functools.partial(_inproj_kernel, n_mix=n_mix, heads=d_sb // HEAD_DIM)
    return pl.pallas_call(
        kern,
        grid=(t // tm, d_in // tn),
        in_specs=[
            pl.BlockSpec((tm, d), lambda i, j: (i, 0)),
            pl.BlockSpec((1, d), lambda i, j: (0, 0)),
            pl.BlockSpec((None, d, tn), lambda i, j: (layer, 0, j)),
            pl.BlockSpec((1, HEAD_DIM), lambda i, j: (0, 0)),
            pl.BlockSpec((1, HEAD_DIM), lambda i, j: (0, 0)),
        ],
        out_specs=[
            pl.BlockSpec((tm, tn), lambda i, j: (i, jnp.minimum(j, n_mix - 1))),
            pl.BlockSpec((tm, tn), lambda i, j: (i, jnp.maximum(j - n_mix, 0))),
        ],
        out_shape=[
            jax.ShapeDtypeStruct((t, n_mix * tn), F32),
            jax.ShapeDtypeStruct((t, 3 * d_sb), BF16),
        ],
        scratch_shapes=[pltpu.VMEM((tm, d), BF16)],
        compiler_params=_params(("arbitrary", "arbitrary")),
        name="in_projection",
    )(x, gain, w_bf16, q_gain, k_gain)


def _softplus(v):
    return jnp.maximum(v, 0.0) + jnp.log1p(jnp.exp(-jnp.abs(v)))


def _sigmoid(v):
    return 0.5 * (jnp.tanh(0.5 * v) + 1.0)


def _causal_conv(ext_scr, cols, new_rows, cw, ts):
    taps = cw.shape[0]
    ext_scr[SUBLANES:SUBLANES + ts, cols] = new_rows
    out = None
    for k in range(taps):
        lo = SUBLANES - (taps - 1) + k
        term = cw[k:k + 1, :] * ext_scr[lo:lo + ts, cols]
        out = term if out is None else out + term
    ext_scr[0:SUBLANES, cols] = ext_scr[ts:ts + SUBLANES, cols]
    return out


def _mixers_kernel(mix_ref, cwa_ref, cba_ref, wg_ref, ba_ref, bx_ref, lam_ref, cwb_ref, mn_ref,
                   out_ref, exta_scr, extb_scr, a_scr, u_scr, hc_scr, *, ts, heads, groups):
    si = pl.program_id(1)
    d_lru = heads * HEAD_DIM
    d_conv = groups * HEAD_DIM

    @pl.when(si == 0)
    def _():
        exta_scr[0:SUBLANES, :] = jnp.zeros((SUBLANES, d_lru), F32)
        extb_scr[0:SUBLANES, :] = jnp.zeros((SUBLANES, d_conv), F32)
        hc_scr[...] = jnp.zeros_like(hc_scr)

    sub = lax.broadcasted_iota(jnp.int32, (ts, HEAD_DIM), 0) & (SUBLANES - 1)

    for h in range(heads):
        cols = slice(h * HEAD_DIM, (h + 1) * HEAD_DIM)
        xc = _causal_conv(exta_scr, cols, mix_ref[0, :, cols], cwa_ref[:, cols], ts) + cba_ref[:, cols]
        gates = jnp.dot(xc.astype(BF16), wg_ref[h], preferred_element_type=F32)
        r = _sigmoid(gates[:, :HEAD_DIM] + ba_ref[:, cols])
        i = _sigmoid(gates[:, HEAD_DIM:] + bx_ref[:, cols])
        log_a = (-LRU_C) * r * _softplus(-lam_ref[:, cols])
        a = jnp.exp(log_a)
        u = jnp.sqrt(-jnp.tanh(log_a) * (a * a + 1.0)) * (i * xc)

        d = 1
        while d < SUBLANES:
            keep = sub >= d
            a_sh = jnp.where(keep, pltpu.roll(a, d, 0), 1.0)
            u_sh = jnp.where(keep, pltpu.roll(u, d, 0), 0.0)
            u = a * u_sh + u
            a = a * a_sh
            d *= 2
        a_scr[...] = a
        u_scr[...] = u

        def carry_step(c, hb):
            off = pl.multiple_of(c * SUBLANES, SUBLANES)
            hrow = u_scr[pl.ds(off, SUBLANES), :] + a_scr[pl.ds(off, SUBLANES), :] * hb
            u_scr[pl.ds(off, SUBLANES), :] = hrow
            return jnp.broadcast_to(hrow[SUBLANES - 1:SUBLANES, :], (SUBLANES, HEAD_DIM))

        hc_scr[:, cols] = lax.fori_loop(0, ts // SUBLANES, carry_step, hc_scr[:, cols], unroll=8)

        y = u_scr[...] * jax.nn.gelu(mix_ref[0, :, d_lru + h * HEAD_DIM:d_lru + (h + 1) * HEAD_DIM])
        out_ref[0, :, cols] = (_rms(y) * mn_ref[:, cols]).astype(BF16)

    for g in range(groups):
        cols = slice(g * HEAD_DIM, (g + 1) * HEAD_DIM)
        col = lambda part: slice(2 * d_lru + part * d_conv + g * HEAD_DIM,
                                 2 * d_lru + part * d_conv + (g + 1) * HEAD_DIM)
        conv = _causal_conv(extb_scr, cols, mix_ref[0, :, col(1)] * mix_ref[0, :, col(2)], cwb_ref[:, cols], ts)
        y = mix_ref[0, :, col(0)] * conv
        ocols = slice(d_lru + g * HEAD_DIM, d_lru + (g + 1) * HEAD_DIM)
        out_ref[0, :, ocols] = (_rms(y) * mn_ref[:, ocols]).astype(BF16)


def _mixers(mix3, conv_a_w, conv_a_b, w_gates, b_a, b_x, lam, conv_b_w, mix_gain):
    b, s, d_mix = mix3.shape
    heads = w_gates.shape[0]
    d_lru = heads * HEAD_DIM
    d_conv = conv_b_w.shape[1]
    groups = d_conv // HEAD_DIM
    assert d_mix == 2 * d_lru + 3 * d_conv
    ts = _tile(s, 512)
    assert max(conv_a_w.shape[0], conv_b_w.shape[0]) - 1 <= SUBLANES and ts % SUBLANES == 0
    kern = functools.partial(_mixers_kernel, ts=ts, heads=heads, groups=groups)
    full = lambda a: pl.BlockSpec(a.shape, lambda bi, si: (0,) * a.ndim)
    return pl.pallas_call(
        kern,
        grid=(b, s // ts),
        in_specs=[pl.BlockSpec((1, ts, d_mix), lambda bi, si: (bi, si, 0)),
                  full(conv_a_w), full(conv_a_b), full(w_gates), full(b_a), full(b_x), full(lam),
                  full(conv_b_w), full(mix_gain)],
        out_specs=pl.BlockSpec((1, ts, d_lru + d_conv), lambda bi, si: (bi, si, 0)),
        out_shape=jax.ShapeDtypeStruct((b, s, d_lru + d_conv), BF16),
        scratch_shapes=[
            pltpu.VMEM((ts + SUBLANES, d_lru), F32),
            pltpu.VMEM((ts + SUBLANES, d_conv), F32),
            pltpu.VMEM((ts, HEAD_DIM), F32),
            pltpu.VMEM((ts, HEAD_DIM), F32),
            pltpu.VMEM((SUBLANES, d_lru), F32),
        ],
        compiler_params=_params(("arbitrary", "arbitrary")),
        name="mixers_rglru_conv",
    )(mix3, conv_a_w, conv_a_b, w_gates, b_a, b_x, lam, conv_b_w, mix_gain)


def _attn_kernel(q_ref, k_ref, v_ref, m_ref, mn_ref, o_ref, acc_scr, run_scr, *, bq, bk, scale):
    qi = pl.program_id(2)
    q = q_ref[0]
    n_sub = bq // bk

    def log_sigmoids(s):
        z = s * scale
        ls = jnp.minimum(z, 0.0) - jnp.log(1.0 + jnp.exp2(jnp.abs(s) * (-scale * LOG2_E)))
        return ls, ls - z

    def suffix_sums(lk):
        hi = lk.astype(BF16)
        lo = (lk - hi.astype(F32)).astype(BF16)
        return jnp.dot(jnp.concatenate([hi, lo], axis=1), m_ref[...], preferred_element_type=F32)

    def chunk(c0, run):
        kc = k_ref[0, pl.ds(c0, bq), :]
        vc = v_ref[0, pl.ds(c0, bq), :]
        s = lax.dot_general(q, kc, (((1,), (1,)), ((), ())), preferred_element_type=F32)
        ws = [None] * n_sub
        for k in reversed(range(n_sub)):
            ls, lk = log_sigmoids(s[:, k * bk:(k + 1) * bk])
            sums = suffix_sums(lk)
            ws[k] = jnp.exp(ls + sums[:, :bk] + run).astype(BF16)
            run = run + sums[:, bk:]
        pv = jnp.dot(jnp.concatenate(ws, axis=1), vc, preferred_element_type=F32)
        return pv, run

    def diagonal_chunk(c0):
        kc = k_ref[0, pl.ds(c0, bq), :]
        vc = v_ref[0, pl.ds(c0, bq), :]
        s = lax.dot_general(q, kc, (((1,), (1,)), ((), ())), preferred_element_type=F32)
        run = jnp.zeros((bq, HEAD_DIM), F32)
        ws = [None] * n_sub
        for k in reversed(range(n_sub)):
            r0 = k * bk
            rows = bq - r0
            ls, lk = log_sigmoids(s[r0:, r0:r0 + bk])
            mask = (lax.broadcasted_iota(jnp.int32, (rows, bk), 1)
                    < lax.broadcasted_iota(jnp.int32, (rows, bk), 0))
            sums = suffix_sums(jnp.where(mask, lk, 0.0))
            w = jnp.where(mask, jnp.exp(ls + sums[:, :bk] + run[r0:]), 0.0).astype(BF16)
            seen = run[r0:] + sums[:, bk:]
            if r0:
                w = jnp.concatenate([jnp.zeros((r0, bk), BF16), w], axis=0)
                seen = jnp.concatenate([run[:r0], seen], axis=0)
            ws[k] = w
            run = seen
        pv = jnp.dot(jnp.concatenate(ws, axis=1), vc, preferred_element_type=F32)
        return pv, run

    acc, run = diagonal_chunk(pl.multiple_of(qi * bq, bq))
    acc_scr[...] = acc
    run_scr[...] = run

    def chunk_pair(p, _):
        pv_a, run_a = chunk(pl.multiple_of((qi - 1 - 2 * p) * bq, bq), run_scr[...])
        pv_b, run_b = chunk(pl.multiple_of((qi - 2 - 2 * p) * bq, bq), run_a)
        acc_scr[...] += pv_a + pv_b
        run_scr[...] = run_b
        return 0

    lax.fori_loop(0, qi // 2, chunk_pair, 0)

    @pl.when(qi % 2 == 1)
    def _():
        pv, _ = chunk(0, run_scr[...])
        acc_scr[...] += pv

    o_ref[0] = (_rms(acc_scr[...]) * mn_ref[...]).astype(BF16)


def _suffix_matrix(bk):
    j = jnp.arange(bk)[:, None]
    s = jnp.arange(bk)[None, :]
    half = jnp.concatenate([(j > s).astype(BF16), jnp.ones((bk, HEAD_DIM), BF16)], axis=1)
    return jnp.concatenate([half, half], axis=0)


def _attention(qkv3, mix_gain, heads):
    b, s, _ = qkv3.shape
    bk = HEAD_DIM
    bq = _tile(s, 512)
    assert bq % bk == 0 and s % bq == 0
    kern = functools.partial(_attn_kernel, bq=bq, bk=bk, scale=1.0 / math.sqrt(HEAD_DIM))
    return pl.pallas_call(
        kern,
        grid=(b, heads, s // bq),
        in_specs=[
            pl.BlockSpec((1, bq, HEAD_DIM), lambda bi, h, qi: (bi, qi, h)),
            pl.BlockSpec((1, s, HEAD_DIM), lambda bi, h, qi: (bi, 0, heads + h)),
            pl.BlockSpec((1, s, HEAD_DIM), lambda bi, h, qi: (bi, 0, 2 * heads + h)),
            pl.BlockSpec((2 * bk, bk + HEAD_DIM), lambda bi, h, qi: (0, 0)),
            pl.BlockSpec((1, HEAD_DIM), lambda bi, h, qi: (0, h)),
        ],
        out_specs=pl.BlockSpec((1, bq, HEAD_DIM), lambda bi, h, qi: (bi, qi, h)),
        out_shape=jax.ShapeDtypeStruct((b, s, heads * HEAD_DIM), BF16),
        scratch_shapes=[pltpu.VMEM((bq, HEAD_DIM), F32), pltpu.VMEM((bq, HEAD_DIM), F32)],
        compiler_params=_params(("arbitrary", "arbitrary", "arbitrary")),
        name="stickbreak_attention",
    )(qkv3, qkv3, qkv3, _suffix_matrix(bk), mix_gain)


def _outproj_kernel(yab_ref, yc_ref, x_ref, w_ref, g_ref, wr_ref, br_ref,
                    x1_ref, h2_ref, lg_ref, *, dab, tm, parts):
    rows = tm // parts
    for p in range(parts):
        rs = slice(p * rows, (p + 1) * rows)
        acc = jnp.dot(yab_ref[rs, :], w_ref[0:dab, :], preferred_element_type=F32)
        acc = acc + jnp.dot(yc_ref[rs, :], w_ref[dab:, :], preferred_element_type=F32)
        x1 = x_ref[rs, :] + acc
        x1_ref[rs, :] = x1
        h2 = (_rms(x1) * g_ref[...]).astype(BF16)
        half = h2.shape[1] // 2
        lo = lax.bitcast_convert_type(h2[:, :half].astype(F32), jnp.int32)
        hi = lax.bitcast_convert_type(h2[:, half:].astype(F32), jnp.int32)
        _store_token_major(h2_ref, p * rows, lax.shift_right_logical(lo, 16) | hi)
        lg = lax.dot_general(wr_ref[...], h2, (((1,), (1,)), ((), ())),
                             preferred_element_type=F32)
        lg_ref[:, rs] = lg + br_ref[...]


def _store_token_major(ref, first_row, val):
    rows, d = val.shape
    rt = d // HEAD_DIM
    for j in range(rt):
        ref[pl.ds(first_row * rt + j, rows, stride=rt), :] = val[:, j * HEAD_DIM:(j + 1) * HEAD_DIM]


def _load_token_major(ref, first_row, rows, rt):
    return [ref[pl.ds(first_row * rt + j, rows, stride=rt), :] for j in range(rt)]


def _out_projection(yab, yc, x, w_bf16, layer, gain, w_route_t, b_route):
    t, d = x.shape
    dab, dc = yab.shape[1], yc.shape[1]
    rt = d // HEAD_DIM
    tm = _tile(t, 512)
    parts = 2 if tm % (2 * HEAD_DIM) == 0 else 1
    kern = functools.partial(_outproj_kernel, dab=dab, tm=tm, parts=parts)
    row = lambda n: pl.BlockSpec((tm, n), lambda i: (i, 0))
    full = lambda shp: pl.BlockSpec(shp, lambda i: (0, 0))
    return pl.pallas_call(
        kern,
        grid=(t // tm,),
        in_specs=[row(dab), row(dc), row(d),
                  pl.BlockSpec((None, dab + dc, d), lambda i: (layer, 0, 0),
                               pipeline_mode=pl.Buffered(1)),
                  full((1, d)),
                  full((ROUTE_ROWS, d)), full((ROUTE_ROWS, 1))],
        out_specs=[row(d), pl.BlockSpec((tm * rt // 2, HEAD_DIM), lambda i: (i, 0)),
                   pl.BlockSpec((ROUTE_ROWS, tm), lambda i: (0, i))],
        out_shape=[jax.ShapeDtypeStruct((t, d), F32), jax.ShapeDtypeStruct((t * rt // 2, HEAD_DIM), jnp.int32),
                   jax.ShapeDtypeStruct((ROUTE_ROWS, t), F32)],
        compiler_params=_params(("arbitrary",)),
        name="out_projection",
    )(yab, yc, x, w_bf16, gain, w_route_t, b_route)


def _route_kernel(lg_ref, tri_ref, ones_ref, ltri_ref, dest_ref, gate_ref, meta_ref,
                  cnt_scr, pst_scr, run_scr, *, n_experts, n_groups, row_block, tl):
    ph = pl.program_id(0)
    i = pl.program_id(1)
    epg = n_experts // n_groups

    @pl.when((ph == 0) & (i == 0))
    def _():
        cnt_scr[...] = jnp.zeros_like(cnt_scr)

    lg = lg_ref[...]
    grp = lg[n_experts:n_experts + n_groups, :]
    gmax = jnp.max(grp, axis=0, keepdims=True)
    giota = lax.broadcasted_iota(jnp.int32, grp.shape, 0)
    gidx = jnp.min(jnp.where(grp == gmax, giota, n_groups), axis=0, keepdims=True)
    p_grp = 1.0 / jnp.sum(jnp.exp(grp - gmax), axis=0, keepdims=True)

    sel = lg[0:epg, :]
    for g in range(1, n_groups):
        sel = jnp.where(gidx == g, lg[g * epg:(g + 1) * epg, :], sel)
    eiota = lax.broadcasted_iota(jnp.int32, sel.shape, 0)
    m1 = jnp.max(sel, axis=0, keepdims=True)
    i1 = jnp.min(jnp.where(sel == m1, eiota, epg), axis=0, keepdims=True)
    rest = jnp.where(eiota == i1, -jnp.inf, sel)
    m2 = jnp.max(rest, axis=0, keepdims=True)
    i2 = jnp.min(jnp.where(rest == m2, eiota, epg), axis=0, keepdims=True)
    e = jnp.exp(m2 - m1)
    den = 1.0 + e
    ex1 = gidx * epg + i1
    ex2 = gidx * epg + i2

    xiota = lax.broadcasted_iota(jnp.int32, (n_experts, tl), 0)
    hit1 = xiota == ex1
    hit2 = xiota == ex2
    onehot = (hit1 | hit2).astype(BF16)
    tile_cnt = jnp.dot(onehot, ones_ref[...], preferred_element_type=F32)

    @pl.when(ph == 0)
    def _():
        cnt_scr[...] += tile_cnt

    @pl.when((ph == 1) & (i == 0))
    def _():
        cnt = cnt_scr[...].astype(jnp.int32)
        nblk = (cnt + (row_block - 1)) // row_block
        pst = jnp.dot(ltri_ref[...], nblk.astype(F32).astype(BF16), preferred_element_type=F32)
        pst_scr[...] = pst
        run_scr[...] = jnp.zeros_like(run_scr)
        meta_ref[0:n_experts, :] = pst.astype(jnp.int32)
        meta_ref[n_experts:2 * n_experts, :] = cnt

    @pl.when(ph == 1)
    def _():
        rank = jnp.dot(onehot, tri_ref[...], preferred_element_type=F32)
        base = pst_scr[...] * float(row_block) + run_scr[...]
        slot = jnp.concatenate([base] * (tl // HEAD_DIM), axis=1) + rank
        d1 = jnp.sum(jnp.where(hit1, slot, 0.0), axis=0, keepdims=True)
        d2 = jnp.sum(jnp.where(hit2, slot, 0.0), axis=0, keepdims=True)
        dest_ref[0:1, :] = d1.astype(jnp.int32)
        dest_ref[1:2, :] = d2.astype(jnp.int32)
        gate_ref[0:1, :] = p_grp * (1.0 / den)
        gate_ref[1:2, :] = p_grp * (e / den)
        run_scr[...] += tile_cnt


def _route(logits_t, n_experts, n_groups, row_block):
    t = logits_t.shape[1]
    tl = _tile(t, 1024)
    assert tl % HEAD_DIM == 0
    a = jnp.arange(tl)
    tri = (a[:, None] < a[None, :]).astype(BF16)
    ones = jnp.ones((tl, HEAD_DIM), BF16)
    ea = jnp.arange(n_experts)
    ltri = (ea[None, :] < ea[:, None]).astype(BF16)
    kern = functools.partial(_route_kernel, n_experts=n_experts, n_groups=n_groups,
                             row_block=row_block, tl=tl)
    full = lambda shp: pl.BlockSpec(shp, lambda p, i: (0, 0))
    return pl.pallas_call(
        kern,
        grid=(2, t // tl),
        in_specs=[pl.BlockSpec((ROUTE_ROWS, tl), lambda p, i: (0, i)),
                  full((tl, tl)), full((tl, HEAD_DIM)), full((n_experts, n_experts))],
        out_specs=[pl.BlockSpec((2, tl), lambda p, i: (0, i * p)),
                   pl.BlockSpec((2, tl), lambda p, i: (0, i * p)),
                   full((2 * n_experts, HEAD_DIM))],
        out_shape=[jax.ShapeDtypeStruct((2, t), jnp.int32), jax.ShapeDtypeStruct((2, t), F32),
                   jax.ShapeDtypeStruct((2 * n_experts, HEAD_DIM), jnp.int32)],
        scratch_shapes=[pltpu.VMEM((n_experts, HEAD_DIM), F32),
                        pltpu.VMEM((n_experts, HEAD_DIM), F32),
                        pltpu.VMEM((n_experts, HEAD_DIM), F32)],
        compiler_params=_params(("arbitrary", "arbitrary")),
        name="moe_route",
    )(logits_t, tri, ones, ltri)


PAD_ROW = -1
DMA_PRIORITIES = 2
X_BUFFERS = 3
Y_BUFFERS = 3
HIGH_HALF = -65536


def _tables_kernel(dest_ref, meta_ref, pad_hbm, src_ref, bexp_ref, nused_ref, *,
                   t, n_experts, n_blocks, row_block):
    pltpu.sync_copy(pad_hbm, src_ref)

    def fill(tok, _):
        src_ref[dest_ref[tok]] = tok
        src_ref[dest_ref[t + tok]] = t + tok
        return 0

    lax.fori_loop(0, t, fill, 0, unroll=8)

    def clear(bi, _):
        bexp_ref[bi] = n_experts - 1
        return 0

    lax.fori_loop(0, n_blocks, clear, 0)

    def per_expert(ex, used):
        first = meta_ref[ex]
        nblk = (meta_ref[n_experts + ex] + (row_block - 1)) // row_block

        def per_block(bj, _):
            bexp_ref[first + bj] = ex
            return 0

        lax.fori_loop(0, nblk, per_block, 0)
        return first + nblk

    nused_ref[0] = lax.fori_loop(0, n_experts, per_expert, 0)


def _tables(dest_flat, meta_flat, t, n_experts, n_blocks, row_block):
    kern = functools.partial(_tables_kernel, t=t, n_experts=n_experts, n_blocks=n_blocks,
                             row_block=row_block)
    smem = pl.BlockSpec(memory_space=pltpu.SMEM)
    pad_rows = jnp.full((n_blocks * row_block,), PAD_ROW, jnp.int32)
    return pl.pallas_call(
        kern,
        in_specs=[smem, smem, pl.BlockSpec(memory_space=pl.ANY)],
        out_specs=[smem, smem, smem],
        out_shape=[jax.ShapeDtypeStruct((n_blocks * row_block,), jnp.int32),
                   jax.ShapeDtypeStruct((n_blocks,), jnp.int32),
                   jax.ShapeDtypeStruct((1,), jnp.int32)],
        name="moe_tables",
    )(dest_flat, meta_flat, pad_rows)


def _expert_kernel(bexp_ref, nused_ref, src_ref, h2_hbm, wg_ref, wu_ref, wd_ref,
                   y_hbm, xbuf, ybuf, wg_bf, wu_bf, wd_bf, gsem, ssem, *, t, rt, row_block, n_blocks):
    i = pl.program_id(0)
    n_used = nused_ref[0]
    xs = i % X_BUFFERS
    ys = i % Y_BUFFERS
    xt = rt // 2

    def buf(ref, s, tiles):
        n = row_block * tiles
        return ref.at[pl.ds(pl.multiple_of(s * n, n), n), :]

    def for_rows(fn):
        def body(rp, _):
            for prio in range(DMA_PRIORITIES):
                fn(rp * DMA_PRIORITIES + prio, prio)
            return 0
        lax.fori_loop(0, row_block // DMA_PRIORITIES, body, 0, unroll=4)

    def start_gather(blk):
        s = blk % X_BUFFERS

        def row(r, prio):
            src = src_ref[blk * row_block + r]
            if t & (t - 1) == 0:
                tok = src & (t - 1)
            else:
                tok = jnp.where(src < 0, 0, jnp.where(src >= t, src - t, src))
            pltpu.make_async_copy(h2_hbm.at[pl.ds(pl.multiple_of(tok * xt, xt), xt), :],
                                  xbuf.at[pl.ds(pl.multiple_of((s * row_block + r) * xt, xt), xt), :],
                                  gsem.at[s]).start(priority=prio)
        for_rows(row)

    def start_scatter(blk):
        s = blk % Y_BUFFERS

        def row(r, prio):
            src = src_ref[blk * row_block + r]
            dst = jnp.where(src < 0, 2 * t + s * row_block + r, src)
            pltpu.make_async_copy(ybuf.at[pl.ds(pl.multiple_of((s * row_block + r) * rt, rt), rt), :],
                                  y_hbm.at[pl.ds(pl.multiple_of(dst * rt, rt), rt), :],
                                  ssem.at[s]).start(priority=prio)
        for_rows(row)

    def wait_gather(s):
        pltpu.make_async_copy(buf(xbuf, s, xt), buf(xbuf, s, xt), gsem.at[s]).wait()

    def wait_scatter(s):
        pltpu.make_async_copy(buf(ybuf, s, rt), buf(ybuf, s, rt), ssem.at[s]).wait()

    @pl.when(i == 0)
    def _():
        ybuf[...] = jnp.zeros_like(ybuf)
        for s in range(Y_BUFFERS):
            spare = y_hbm.at[pl.ds((2 * t + s * row_block) * rt, row_block * rt), :]
            pltpu.make_async_copy(buf(ybuf, s, rt), spare, ssem.at[s]).start()
        for s in range(Y_BUFFERS):
            wait_scatter(s)
        for first in range(X_BUFFERS - 1):
            @pl.when(first < n_used)
            def _():
                start_gather(first)

    @pl.when((i == 0) | (bexp_ref[i] != bexp_ref[jnp.maximum(i - 1, 0)]))
    def _():
        wg_bf[...] = wg_ref[...].astype(BF16)
        wu_bf[...] = wu_ref[...].astype(BF16)
        wd_bf[...] = wd_ref[...].astype(BF16)

    @pl.when(i + (X_BUFFERS - 1) < n_used)
    def _():
        start_gather(i + (X_BUFFERS - 1))

    @pl.when(i < n_used)
    def _():
        wait_gather(xs)

        @pl.when(i >= Y_BUFFERS)
        def _():
            wait_scatter(ys)

        words = _load_token_major(xbuf, xs * row_block, row_block, xt)
        lo = [lax.bitcast_convert_type(w << 16, F32).astype(BF16) for w in words]
        hi = [lax.bitcast_convert_type(w & HIGH_HALF, F32).astype(BF16) for w in words]
        x = jnp.concatenate(lo + hi, axis=1)
        gt = jnp.dot(x, wg_bf[...], preferred_element_type=F32)
        up = jnp.dot(x, wu_bf[...], preferred_element_type=F32)
        hid = (jax.nn.silu(gt) * up).astype(BF16)
        _store_token_major(ybuf, ys * row_block, jnp.dot(hid, wd_bf[...], preferred_element_type=F32))
        start_scatter(i)

    @pl.when(i == n_blocks - 1)
    def _():
        for back in range(Y_BUFFERS):
            @pl.when(n_used > back)
            def _():
                wait_scatter((n_used - 1 - back) % Y_BUFFERS)


def _experts(h2_tm, t, w_gate, w_up, w_down, layer, bexp, nused, src, row_block):
    d = w_gate.shape[2]
    rt = d // HEAD_DIM
    assert h2_tm.shape == (t * rt // 2, HEAD_DIM) and h2_tm.dtype == jnp.int32
    n_blocks = bexp.shape[0]
    de = w_gate.shape[3]
    kern = functools.partial(_expert_kernel, t=t, rt=rt, row_block=row_block, n_blocks=n_blocks)
    grid_spec = pltpu.PrefetchScalarGridSpec(
        num_scalar_prefetch=3,
        grid=(n_blocks,),
        in_specs=[
            pl.BlockSpec(memory_space=pl.ANY),
            pl.BlockSpec((None, None, d, de), lambda i, be, nu, sr: (layer, be[i], 0, 0)),
            pl.BlockSpec((None, None, d, de), lambda i, be, nu, sr: (layer, be[i], 0, 0)),
            pl.BlockSpec((None, None, de, d), lambda i, be, nu, sr: (layer, be[i], 0, 0)),
        ],
        out_specs=pl.BlockSpec(memory_space=pl.ANY),
        scratch_shapes=[
            pltpu.VMEM((X_BUFFERS * row_block * rt // 2, HEAD_DIM), jnp.int32),
            pltpu.VMEM((Y_BUFFERS * row_block * rt, HEAD_DIM), F32),
            pltpu.VMEM((d, de), BF16),
            pltpu.VMEM((d, de), BF16),
            pltpu.VMEM((de, d), BF16),
            pltpu.SemaphoreType.DMA((X_BUFFERS,)),
            pltpu.SemaphoreType.DMA((Y_BUFFERS,)),
        ],
    )
    return pl.pallas_call(
        kern,
        grid_spec=grid_spec,
        out_shape=jax.ShapeDtypeStruct(((2 * t + Y_BUFFERS * row_block) * rt, HEAD_DIM), F32),
        compiler_params=_params(("arbitrary",)),
        name="moe_experts",
    )(bexp, nused, src, h2_tm, w_gate, w_up, w_down)


def _combine_kernel(x_ref, y0_ref, y1_ref, g_ref, o_ref, *, tm, rt):
    g = g_ref[...]
    g0 = jnp.broadcast_to(g[:, 0:1], (tm, HEAD_DIM))
    g1 = jnp.broadcast_to(g[:, 1:2], (tm, HEAD_DIM))
    for j in range(rt):
        sl = slice(j * HEAD_DIM, (j + 1) * HEAD_DIM)
        y0 = y0_ref[pl.ds(j, tm, stride=rt), :]
        y1 = y1_ref[pl.ds(j, tm, stride=rt), :]
        o_ref[:, sl] = x_ref[:, sl] + (y0 * g0 + y1 * g1)


def _combine(x1, y_tok, gates_t):
    t, d = x1.shape
    rt = d // HEAD_DIM
    tm = _tile(t, 512)
    nb = t // tm
    return pl.pallas_call(
        functools.partial(_combine_kernel, tm=tm, rt=rt),
        grid=(nb,),
        in_specs=[pl.BlockSpec((tm, d), lambda i: (i, 0)),
                  pl.BlockSpec((tm * rt, HEAD_DIM), lambda i: (i, 0)),
                  pl.BlockSpec((tm * rt, HEAD_DIM), lambda i: (i + nb, 0)),
                  pl.BlockSpec((tm, 2), lambda i: (i, 0))],
        out_specs=pl.BlockSpec((tm, d), lambda i: (i, 0)),
        out_shape=jax.ShapeDtypeStruct((t, d), F32),
        compiler_params=_params(("arbitrary",)),
        name="moe_combine",
    )(x1, y_tok, y_tok, gates_t)


MOE_ROW_BLOCK = 256


def kernel(x, norm_mix, w_in, conv_a_w, conv_a_b, lru_wa, lru_ba, lru_wx, lru_bx, lru_lam,
           conv_b_w, q_norm, k_norm, mix_norm, w_out, norm_ffn, w_router_group, b_router_group,
           w_router_expert, b_router_expert, w_gate, w_up, w_down):
    bsz, s, d = x.shape
    t = bsz * s
    depth = w_in.shape[0]
    lru_heads = lru_wa.shape[1]
    d_lru = lru_heads * HEAD_DIM
    d_conv = conv_b_w.shape[-1]
    d_sb =(w_in.shape[-1] - 2 * d_lru - 3 * d_conv) // 3
    sb_heads = d_sb // HEAD_DIM
    n_groups = w_router_group.shape[-1]
    n_experts = w_router_expert.shape[-1]
    assert n_experts // n_groups == EXPERTS_PER_GROUP and n_experts + n_groups <= ROUTE_ROWS
    row_block = MOE_ROW_BLOCK
    n_blocks = -(-(2 * t) // row_block) + n_experts
    assert -(-t // row_block) <= 256

    xt = x.reshape(t, d)
    w_in_bf = w_in.astype(BF16)
    w_out_bf = w_out.astype(BF16)
    for l in range(depth):
        row = lambda v: v[l].reshape(1, -1)
        mix, qkv = _in_projection(xt, row(norm_mix), w_in_bf, l, row(q_norm), row(k_norm), d_sb)
        mix3 = mix.reshape(bsz, s, -1)
        qkv3 = qkv.reshape(bsz, s, -1)
        mg = row(mix_norm)
        w_gates = jnp.concatenate([lru_wa[l], lru_wx[l]], axis=-1).astype(BF16)
        yab = _mixers(mix3, conv_a_w[l], row(conv_a_b), w_gates, row(lru_ba), row(lru_bx),
                      row(lru_lam), conv_b_w[l], mg[:, :d_lru + d_conv])
        yc = _attention(qkv3, mg[:, d_lru + d_conv:], sb_heads)

        pad = jnp.zeros((d, ROUTE_ROWS - n_experts - n_groups), F32)
        w_route_t = jnp.concatenate([w_router_expert[l], w_router_group[l], pad], axis=1).T.astype(BF16)
        b_route = jnp.concatenate([b_router_expert[l], b_router_group[l],
                                   jnp.zeros((ROUTE_ROWS - n_experts - n_groups,), F32)]).reshape(-1, 1)
        x1, h2, logits_t = _out_projection(yab.reshape(t, -1), yc.reshape(t, -1), xt,
                                           w_out_bf, l, row(norm_ffn), w_route_t, b_route)

        dest, gates, meta = _route(logits_t, n_experts, n_groups, row_block)
        src, bexp, nused = _tables(dest.reshape(-1), meta[:, 0], t, n_experts, n_blocks, row_block)
        y_tok = _experts(h2, t, w_gate, w_up, w_down, l, bexp, nused, src, row_block)
        xt = _combine(x1, y_tok, gates.T)
    return xt.reshape(bsz, s, d)
```

```python
import functools
import math

import jax
import jax.numpy as jnp
from jax import lax
from jax.experimental import pallas as pl
from jax.experimental.pallas import tpu as pltpu

EPS = 1e-6
LRU_C = 8.0
LOG2_E = 1.4426950408889634
HEAD_DIM = 128
SUBLANES = 8
EXPERTS_PER_GROUP = 8
ROUTE_ROWS = 48
VMEM_LIMIT = 56 * 1024 * 1024

F32 = jnp.float32
BF16 = jnp.bfloat16


def _tile(n, pref):
    if n <= pref:
        return n
    t = pref
    while n % t:
        t -= SUBLANES
    assert t > 0
    return t


def _params(sem):
    return pltpu.CompilerParams(dimension_semantics=sem, vmem_limit_bytes=VMEM_LIMIT)


def _rms(v):
    return v * lax.rsqrt(jnp.mean(v * v, axis=-1, keepdims=True) + EPS)


def _inproj_kernel(x_ref, g_ref, w_ref, qn_ref, kn_ref, mix_ref, qkv_ref, h_scr, *, n_mix, heads):
    j = pl.program_id(1)

    @pl.when(j == 0)
    def _():
        h_scr[...] = (_rms(x_ref[...]) * g_ref[...]).astype(BF16)

    acc = jnp.dot(h_scr[...], w_ref[...], preferred_element_type=F32)

    @pl.when(j < n_mix)
    def _():
        mix_ref[...] = acc

    def head_norm(gain_ref):
        for h in range(heads):
            sl = slice(h * HEAD_DIM, (h + 1) * HEAD_DIM)
            qkv_ref[:, sl] = (_rms(acc[:, sl]) * gain_ref[...]).astype(BF16)

    @pl.when(j == n_mix)
    def _():
        head_norm(qn_ref)

    @pl.when(j == n_mix + 1)
    def _():
        head_norm(kn_ref)

    @pl.when(j == n_mix + 2)
    def _():
        qkv_ref[...] = acc.astype(BF16)


def _in_projection(x, gain, w_bf16, layer, q_gain, k_gain, d_sb):
    t, d = x.shape
    d_in = w_bf16.shape[2]
    tn = d_sb
    assert d_in % tn == 0 and (d_in - 3 * d_sb) % tn == 0
    n_mix = (d_in - 3 * d_sb) // tn
    tm = _tile(t, 1024)
    kern = functools.partial(_inproj_kernel, n_mix=n_mix, heads=d_sb // HEAD_DIM)
    return pl.pallas_call(
        kern,
        grid=(t // tm, d_in // tn),
        in_specs=[
            pl.BlockSpec((tm, d), lambda i, j: (i, 0)),
            pl.BlockSpec((1, d), lambda i, j: (0, 0)),
            pl.BlockSpec((None, d, tn), lambda i, j: (layer, 0, j)),
            pl.BlockSpec((1, HEAD_DIM), lambda i, j: (0, 0)),
            pl.BlockSpec((1, HEAD_DIM), lambda i, j: (0, 0)),
        ],
        out_specs=[
            pl.BlockSpec((tm, tn), lambda i, j: (i, jnp.minimum(j, n_mix - 1))),
            pl.BlockSpec((tm, tn), lambda i, j: (i, jnp.maximum(j - n_mix, 0))),
        ],
        out_shape=[
            jax.ShapeDtypeStruct((t, n_mix * tn), F32),
            jax.ShapeDtypeStruct((t, 3 * d_sb), BF16),
        ],
        scratch_shapes=[pltpu.VMEM((tm, d), BF16)],
        compiler_params=_params(("arbitrary", "arbitrary")),
        name="in_projection",
    )(x, gain, w_bf16, q_gain, k_gain)


def _softplus(v):
    return jnp.maximum(v, 0.0) + jnp.log1p(jnp.exp(-jnp.abs(v)))


def _sigmoid(v):
    return 0.5 * (jnp.tanh(0.5 * v) + 1.0)


def _causal_conv(ext_scr, cols, new_rows, cw, ts):
    taps = cw.shape[0]
    ext_scr[SUBLANES:SUBLANES + ts, cols] = new_rows
    out = None
    for k in range(taps):
        lo = SUBLANES - (taps - 1) + k
        term = cw[k:k + 1, :] * ext_scr[lo:lo + ts, cols]
        out = term if out is None else out + term
    ext_scr[0:SUBLANES, cols] = ext_scr[ts:ts + SUBLANES, cols]
    return out


def _mixers_kernel(mix_ref, cwa_ref, cba_ref, wg_ref, ba_ref, bx_ref, lam_ref, cwb_ref, mn_ref,
                   out_ref, exta_scr, extb_scr, a_scr, u_scr, hc_scr, *, ts, heads, groups):
    si = pl.program_id(1)
    d_lru = heads * HEAD_DIM
    d_conv = groups * HEAD_DIM

    @pl.when(si == 0)
    def _():
        exta_scr[0:SUBLANES, :] = jnp.zeros((SUBLANES, d_lru), F32)
        extb_scr[0:SUBLANES, :] = jnp.zeros((SUBLANES, d_conv), F32)
        hc_scr[...] = jnp.zeros_like(hc_scr)

    sub = lax.broadcasted_iota(jnp.int32, (ts, HEAD_DIM), 0) & (SUBLANES - 1)

    for h in range(heads):
        cols = slice(h * HEAD_DIM, (h + 1) * HEAD_DIM)
        xc = _causal_conv(exta_scr, cols, mix_ref[0, :, cols], cwa_ref[:, cols], ts) + cba_ref[:, cols]
        gates = jnp.dot(xc.astype(BF16), wg_ref[h], preferred_element_type=F32)
        r = _sigmoid(gates[:, :HEAD_DIM] + ba_ref[:, cols])
        i = _sigmoid(gates[:, HEAD_DIM:] + bx_ref[:, cols])
        log_a = (-LRU_C) * r * _softplus(-lam_ref[:, cols])
        a = jnp.exp(log_a)
        u = jnp.sqrt(-jnp.tanh(log_a) * (a * a + 1.0)) * (i * xc)

        d = 1
        while d < SUBLANES:
            keep = sub >= d
            a_sh = jnp.where(keep, pltpu.roll(a, d, 0), 1.0)
            u_sh = jnp.where(keep, pltpu.roll(u, d, 0), 0.0)
            u = a * u_sh + u
            a = a * a_sh
            d *= 2
        a_scr[...] = a
        u_scr[...] = u

        def carry_step(c, hb):
            off = pl.multiple_of(c * SUBLANES, SUBLANES)
            hrow = u_scr[pl.ds(off, SUBLANES), :] + a_scr[pl.ds(off, SUBLANES), :] * hb
            u_scr[pl.ds(off, SUBLANES), :] = hrow
            return jnp.broadcast_to(hrow[SUBLANES - 1:SUBLANES, :], (SUBLANES, HEAD_DIM))

        hc_scr[:, cols] = lax.fori_loop(0, ts // SUBLANES, carry_step, hc_scr[:, cols], unroll=8)

        y = u_scr[...] * jax.nn.gelu(mix_ref[0, :, d_lru + h * HEAD_DIM:d_lru + (h + 1) * HEAD_DIM])
        out_ref[0, :, cols] = (_rms(y) * mn_ref[:, cols]).astype(BF16)

    for g in range(groups):
        cols = slice(g * HEAD_DIM, (g + 1) * HEAD_DIM)
        col = lambda part: slice(2 * d_lru + part * d_conv + g * HEAD_DIM,
                                 2 * d_lru + part * d_conv + (g + 1) * HEAD_DIM)
        conv = _causal_conv(extb_scr, cols, mix_ref[0, :, col(1)] * mix_ref[0, :, col(2)], cwb_ref[:, cols], ts)
        y = mix_ref[0, :, col(0)] * conv
        ocols = slice(d_lru + g * HEAD_DIM, d_lru + (g + 1) * HEAD_DIM)
        out_ref[0, :, ocols] = (_rms(y) * mn_ref[:, ocols]).astype(BF16)


def _mixers(mix3, conv_a_w, conv_a_b, w_gates, b_a, b_x, lam, conv_b_w, mix_gain):
    b, s, d_mix = mix3.shape
    heads = w_gates.shape[0]
    d_lru = heads * HEAD_DIM
    d_conv = conv_b_w.shape[1]
    groups = d_conv // HEAD_DIM
    assert d_mix == 2 * d_lru + 3 * d_conv
    ts = _tile(s, 512)
    assert max(conv_a_w.shape[0], conv_b_w.shape[0]) - 1 <= SUBLANES and ts % SUBLANES == 0
    kern = functools.partial(_mixers_kernel, ts=ts, heads=heads, groups=groups)
    full = lambda a: pl.BlockSpec(a.shape, lambda bi, si: (0,) * a.ndim)
    return pl.pallas_call(
        kern,
        grid=(b, s // ts),
        in_specs=[pl.BlockSpec((1, ts, d_mix), lambda bi, si: (bi, si, 0)),
                  full(conv_a_w), full(conv_a_b), full(w_gates), full(b_a), full(b_x), full(lam),
                  full(conv_b_w), full(mix_gain)],
        out_specs=pl.BlockSpec((1, ts, d_lru + d_conv), lambda bi, si: (bi, si, 0)),
        out_shape=jax.ShapeDtypeStruct((b, s, d_lru + d_conv), BF16),
        scratch_shapes=[
            pltpu.VMEM((ts + SUBLANES, d_lru), F32),
            pltpu.VMEM((ts + SUBLANES, d_conv), F32),
            pltpu.VMEM((ts, HEAD_DIM), F32),
            pltpu.VMEM((ts, HEAD_DIM), F32),
            pltpu.VMEM((SUBLANES, d_lru), F32),
        ],
        compiler_params=_params(("arbitrary", "arbitrary")),
        name="mixers_rglru_conv",
    )(mix3, conv_a_w, conv_a_b, w_gates, b_a, b_x, lam, conv_b_w, mix_gain)


def _attn_kernel(q_ref, k_ref, v_ref, m_ref, mn_ref, o_ref, acc_scr, run_scr, *, bq, bk, scale):
    qi = pl.program_id(2)
    q = q_ref[0]
    n_sub = bq // bk

    def log_sigmoids(s):
        z = s * scale
        ls = jnp.minimum(z, 0.0) - jnp.log(1.0 + jnp.exp2(jnp.abs(s) * (-scale * LOG2_E)))
        return ls, ls - z

    def suffix_sums(lk):
        hi = lk.astype(BF16)
        lo = (lk - hi.astype(F32)).astype(BF16)
        return jnp.dot(jnp.concatenate([hi, lo], axis=1), m_ref[...], preferred_element_type=F32)

    def chunk(c0, run):
        kc = k_ref[0, pl.ds(c0, bq), :]
        vc = v_ref[0, pl.ds(c0, bq), :]
        s = lax.dot_general(q, kc, (((1,), (1,)), ((), ())), preferred_element_type=F32)
        ws = [None] * n_sub
        for k in reversed(range(n_sub)):
            ls, lk = log_sigmoids(s[:, k * bk:(k + 1) * bk])
            sums = suffix_sums(lk)
            ws[k] = jnp.exp(ls + sums[:, :bk] + run).astype(BF16)
            run = run + sums[:, bk:]
        pv = jnp.dot(jnp.concatenate(ws, axis=1), vc, preferred_element_type=F32)
        return pv, run

    def diagonal_chunk(c0):
        kc = k_ref[0, pl.ds(c0, bq), :]
        vc = v_ref[0, pl.ds(c0, bq), :]
        s = lax.dot_general(q, kc, (((1,), (1,)), ((), ())), preferred_element_type=F32)
        run = jnp.zeros((bq, HEAD_DIM), F32)
        ws = [None] * n_sub
        for k in reversed(range(n_sub)):
            r0 = k * bk
            rows = bq - r0
            ls, lk = log_sigmoids(s[r0:, r0:r0 + bk])
            mask = (lax.broadcasted_iota(jnp.int32, (rows, bk), 1)
                    < lax.broadcasted_iota(jnp.int32, (rows, bk), 0))
            sums = suffix_sums(jnp.where(mask, lk, 0.0))
            w = jnp.where(mask, jnp.exp(ls + sums[:, :bk] + run[r0:]), 0.0).astype(BF16)
            seen = run[r0:] + sums[:, bk:]
            if r0:
                w = jnp.concatenate([jnp.zeros((r0, bk), BF16), w], axis=0)
                seen = jnp.concatenate([run[:r0], seen], axis=0)
            ws[k] = w
            run = seen
        pv = jnp.dot(jnp.concatenate(ws, axis=1), vc, preferred_element_type=F32)
        return pv, run

    acc, run = diagonal_chunk(pl.multiple_of(qi * bq, bq))
    acc_scr[...] = acc
    run_scr[...] = run

    def chunk_pair(p, _):
        pv_a, run_a = chunk(pl.multiple_of((qi - 1 - 2 * p) * bq, bq), run_scr[...])
        pv_b, run_b = chunk(pl.multiple_of((qi - 2 - 2 * p) * bq, bq), run_a)
        acc_scr[...] += pv_a + pv_b
        run_scr[...] = run_b
        return 0

    lax.fori_loop(0, qi // 2, chunk_pair, 0)

    @pl.when(qi % 2 == 1)
    def _():
        pv, _ = chunk(0, run_scr[...])
        acc_scr[...] += pv

    o_ref[0] = (_rms(acc_scr[...]) * mn_ref[...]).astype(BF16)


def _suffix_matrix(bk):
    j = jnp.arange(bk)[:, None]
    s = jnp.arange(bk)[None, :]
    half = jnp.concatenate([(j > s).astype(BF16), jnp.ones((bk, HEAD_DIM), BF16)], axis=1)
    return jnp.concatenate([half, half], axis=0)


def _attention(qkv3, mix_gain, heads):
    b, s, _ = qkv3.shape
    bk = HEAD_DIM
    bq = _tile(s, 512)
    assert bq % bk == 0 and s % bq == 0
    kern = functools.partial(_attn_kernel, bq=bq, bk=bk, scale=1.0 / math.sqrt(HEAD_DIM))
    return pl.pallas_call(
        kern,
        grid=(b, heads, s // bq),
        in_specs=[
            pl.BlockSpec((1, bq, HEAD_DIM), lambda bi, h, qi: (bi, qi, h)),
            pl.BlockSpec((1, s, HEAD_DIM), lambda bi, h, qi: (bi, 0, heads + h)),
            pl.BlockSpec((1, s, HEAD_DIM), lambda bi, h, qi: (bi, 0, 2 * heads + h)),
            pl.BlockSpec((2 * bk, bk + HEAD_DIM), lambda bi, h, qi: (0, 0)),
            pl.BlockSpec((1, HEAD_DIM), lambda bi, h, qi: (0, h)),
        ],
        out_specs=pl.BlockSpec((1, bq, HEAD_DIM), lambda bi, h, qi: (bi, qi, h)),
        out_shape=jax.ShapeDtypeStruct((b, s, heads * HEAD_DIM), BF16),
        scratch_shapes=[pltpu.VMEM((bq, HEAD_DIM), F32), pltpu.VMEM((bq, HEAD_DIM), F32)],
        compiler_params=_params(("arbitrary", "arbitrary", "arbitrary")),
        name="stickbreak_attention",
    )(qkv3, qkv3, qkv3, _suffix_matrix(bk), mix_gain)


def _outproj_kernel(yab_ref, yc_ref, x_ref, w_ref, g_ref, wr_ref, br_ref,
                    x1_ref, h2_ref, lg_ref, *, dab, tm, parts):
    rows = tm // parts
    for p in range(parts):
        rs = slice(p * rows, (p + 1) * rows)
        acc = jnp.dot(yab_ref[rs, :], w_ref[0:dab, :], preferred_element_type=F32)
        acc = acc + jnp.dot(yc_ref[rs, :], w_ref[dab:, :], preferred_element_type=F32)
        x1 = x_ref[rs, :] + acc
        x1_ref[rs, :] = x1
        h2 = (_rms(x1) * g_ref[...]).astype(BF16)
        half = h2.shape[1] // 2
        lo = lax.bitcast_convert_type(h2[:, :half].astype(F32), jnp.int32)
        hi = lax.bitcast_convert_type(h2[:, half:].astype(F32), jnp.int32)
        _store_token_major(h2_ref, p * rows, lax.shift_right_logical(lo, 16) | hi)
        lg = lax.dot_general(wr_ref[...], h2, (((1,), (1,)), ((), ())),
                             preferred_element_type=F32)
        lg_ref[:, rs] = lg + br_ref[...]


def _store_token_major(ref, first_row, val):
    rows, d = val.shape
    rt = d // HEAD_DIM
    for j in range(rt):
        ref[pl.ds(first_row * rt + j, rows, stride=rt), :] = val[:, j * HEAD_DIM:(j + 1) * HEAD_DIM]


def _load_token_major(ref, first_row, rows, rt):
    return [ref[pl.ds(first_row * rt + j, rows, stride=rt), :] for j in range(rt)]


def _out_projection(yab, yc, x, w_bf16, layer, gain, w_route_t, b_route):
    t, d = x.shape
    dab, dc = yab.shape[1], yc.shape[1]
    rt = d // HEAD_DIM
    tm = _tile(t, 512)
    parts = 2 if tm % (2 * HEAD_DIM) == 0 else 1
    kern = functools.partial(_outproj_kernel, dab=dab, tm=tm, parts=parts)
    row = lambda n: pl.BlockSpec((tm, n), lambda i: (i, 0))
    full = lambda shp: pl.BlockSpec(shp, lambda i: (0, 0))
    return pl.pallas_call(
        kern,
        grid=(t // tm,),
        in_specs=[row(dab), row(dc), row(d),
                  pl.BlockSpec((None, dab + dc, d), lambda i: (layer, 0, 0),
                               pipeline_mode=pl.Buffered(1)),
                  full((1, d)),
                  full((ROUTE_ROWS, d)), full((ROUTE_ROWS, 1))],
        out_specs=[row(d), pl.BlockSpec((tm * rt // 2, HEAD_DIM), lambda i: (i, 0)),
                   pl.BlockSpec((ROUTE_ROWS, tm), lambda i: (0, i))],
        out_shape=[jax.ShapeDtypeStruct((t, d), F32), jax.ShapeDtypeStruct((t * rt // 2, HEAD_DIM), jnp.int32),
                   jax.ShapeDtypeStruct((ROUTE_ROWS, t), F32)],
        compiler_params=_params(("arbitrary",)),
        name="out_projection",
    )(yab, yc, x, w_bf16, gain, w_route_t, b_route)


def _route_kernel(lg_ref, tri_ref, ones_ref, ltri_ref, dest_ref, gate_ref, meta_ref,
                  cnt_scr, pst_scr, run_scr, *, n_experts, n_groups, row_block, tl):
    ph = pl.program_id(0)
    i = pl.program_id(1)
    epg = n_experts // n_groups

    @pl.when((ph == 0) & (i == 0))
    def _():
        cnt_scr[...] = jnp.zeros_like(cnt_scr)

    lg = lg_ref[...]
    grp = lg[n_experts:n_experts + n_groups, :]
    gmax = jnp.max(grp, axis=0, keepdims=True)
    giota = lax.broadcasted_iota(jnp.int32, grp.shape, 0)
    gidx = jnp.min(jnp.where(grp == gmax, giota, n_groups), axis=0, keepdims=True)
    p_grp = 1.0 / jnp.sum(jnp.exp(grp - gmax), axis=0, keepdims=True)

    sel = lg[0:epg, :]
    for g in range(1, n_groups):
        sel = jnp.where(gidx == g, lg[g * epg:(g + 1) * epg, :], sel)
    eiota = lax.broadcasted_iota(jnp.int32, sel.shape, 0)
    m1 = jnp.max(sel, axis=0, keepdims=True)
    i1 = jnp.min(jnp.where(sel == m1, eiota, epg), axis=0, keepdims=True)
    rest = jnp.where(eiota == i1, -jnp.inf, sel)
    m2 = jnp.max(rest, axis=0, keepdims=True)
    i2 = jnp.min(jnp.where(rest == m2, eiota, epg), axis=0, keepdims=True)
    e = jnp.exp(m2 - m1)
    den = 1.0 + e
    ex1 = gidx * epg + i1
    ex2 = gidx * epg + i2

    xiota = lax.broadcasted_iota(jnp.int32, (n_experts, tl), 0)
    hit1 = xiota == ex1
    hit2 = xiota == ex2
    onehot = (hit1 | hit2).astype(BF16)
    tile_cnt = jnp.dot(onehot, ones_ref[...], preferred_element_type=F32)

    @pl.when(ph == 0)
    def _():
        cnt_scr[...] += tile_cnt

    @pl.when((ph == 1) & (i == 0))
    def _():
        cnt = cnt_scr[...].astype(jnp.int32)
        nblk = (cnt + (row_block - 1)) // row_block
        pst = jnp.dot(ltri_ref[...], nblk.astype(F32).astype(BF16), preferred_element_type=F32)
        pst_scr[...] = pst
        run_scr[...] = jnp.zeros_like(run_scr)
        meta_ref[0:n_experts, :] = pst.astype(jnp.int32)
        meta_ref[n_experts:2 * n_experts, :] = cnt

    @pl.when(ph == 1)
    def _():
        rank = jnp.dot(onehot, tri_ref[...], preferred_element_type=F32)
        base = pst_scr[...] * float(row_block) + run_scr[...]
        slot = jnp.concatenate([base] * (tl // HEAD_DIM), axis=1) + rank
        d1 = jnp.sum(jnp.where(hit1, slot, 0.0), axis=0, keepdims=True)
        d2 = jnp.sum(jnp.where(hit2, slot, 0.0), axis=0, keepdims=True)
        dest_ref[0:1, :] = d1.astype(jnp.int32)
        dest_ref[1:2, :] = d2.astype(jnp.int32)
        gate_ref[0:1, :] = p_grp * (1.0 / den)
        gate_ref[1:2, :] = p_grp * (e / den)
        run_scr[...] += tile_cnt


def _route(logits_t, n_experts, n_groups, row_block):
    t = logits_t.shape[1]
    tl = _tile(t, 1024)
    assert tl % HEAD_DIM == 0
    a = jnp.arange(tl)
    tri = (a[:, None] < a[None, :]).astype(BF16)
    ones = jnp.ones((tl, HEAD_DIM), BF16)
    ea = jnp.arange(n_experts)
    ltri = (ea[None, :] < ea[:, None]).astype(BF16)
    kern = functools.partial(_route_kernel, n_experts=n_experts, n_groups=n_groups,
                             row_block=row_block, tl=tl)
    full = lambda shp: pl.BlockSpec(shp, lambda p, i: (0, 0))
    return pl.pallas_call(
        kern,
        grid=(2, t // tl),
        in_specs=[pl.BlockSpec((ROUTE_ROWS, tl), lambda p, i: (0, i)),
                  full((tl, tl)), full((tl, HEAD_DIM)), full((n_experts, n_experts))],
        out_specs=[pl.BlockSpec((2, tl), lambda p, i: (0, i * p)),
                   pl.BlockSpec((2, tl), lambda p, i: (0, i * p)),
                   full((2 * n_experts, HEAD_DIM))],
        out_shape=[jax.ShapeDtypeStruct((2, t), jnp.int32), jax.ShapeDtypeStruct((2, t), F32),
                   jax.ShapeDtypeStruct((2 * n_experts, HEAD_DIM), jnp.int32)],
        scratch_shapes=[pltpu.VMEM((n_experts, HEAD_DIM), F32),
                        pltpu.VMEM((n_experts, HEAD_DIM), F32),
                        pltpu.VMEM((n_experts, HEAD_DIM), F32)],
        compiler_params=_params(("arbitrary", "arbitrary")),
        name="moe_route",
    )(logits_t, tri, ones, ltri)


PAD_ROW = -1
DMA_PRIORITIES = 2
X_BUFFERS = 4
Y_BUFFERS = 3
HIGH_HALF = -65536


def _tables_kernel(dest_ref, meta_ref, pad_hbm, src_ref, bexp_ref, nused_ref, *,
                   t, n_experts, n_blocks, row_block):
    pltpu.sync_copy(pad_hbm, src_ref)

    def fill(tok, _):
        src_ref[dest_ref[tok]] = tok
        src_ref[dest_ref[t + tok]] = t + tok
        return 0

    lax.fori_loop(0, t, fill, 0, unroll=16)

    def clear(bi, _):
        bexp_ref[bi] = n_experts - 1
        return 0

    lax.fori_loop(0, n_blocks, clear, 0)

    def per_expert(ex, used):
        first = meta_ref[ex]
        nblk = (meta_ref[n_experts + ex] + (row_block - 1)) // row_block

        def per_block(bj, _):
            bexp_ref[first + bj] = ex
            return 0

        lax.fori_loop(0, nblk, per_block, 0)
        return first + nblk

    nused_ref[0] = lax.fori_loop(0, n_experts, per_expert, 0)


def _tables(dest_flat, meta_flat, t, n_experts, n_blocks, row_block):
    kern = functools.partial(_tables_kernel, t=t, n_experts=n_experts, n_blocks=n_blocks,
                             row_block=row_block)
    smem = pl.BlockSpec(memory_space=pltpu.SMEM)
    pad_rows = jnp.full((n_blocks * row_block,), PAD_ROW, jnp.int32)
    return pl.pallas_call(
        kern,
        in_specs=[smem, smem, pl.BlockSpec(memory_space=pl.ANY)],
        out_specs=[smem, smem, smem],
        out_shape=[jax.ShapeDtypeStruct((n_blocks * row_block,), jnp.int32),
                   jax.ShapeDtypeStruct((n_blocks,), jnp.int32),
                   jax.ShapeDtypeStruct((1,), jnp.int32)],
        name="moe_tables",
    )(dest_flat, meta_flat, pad_rows)


def _expert_kernel(bexp_ref, nused_ref, src_ref, h2_hbm, wg_ref, wu_ref, wd_ref,
                   y_hbm, xbuf, ybuf, wg_bf, wu_bf, wd_bf, gsem, ssem, *, t, rt, row_block, n_blocks):
    i = pl.program_id(0)
    n_used = nused_ref[0]
    xs = i % X_BUFFERS
    ys = i % Y_BUFFERS
    xt = rt // 2

    def buf(ref, s, tiles):
        n = row_block * tiles
        return ref.at[pl.ds(pl.multiple_of(s * n, n), n), :]

    def for_rows(fn):
        def body(rp, _):
            for prio in range(DMA_PRIORITIES):
                fn(rp * DMA_PRIORITIES + prio, prio)
            return 0
        lax.fori_loop(0, row_block // DMA_PRIORITIES, body, 0, unroll=4)

    def start_gather(blk):
        s = blk % X_BUFFERS

        def row(r, prio):
            src = src_ref[blk * row_block + r]
            if t & (t - 1) == 0:
                tok = src & (t - 1)
            else:
                tok = jnp.where(src < 0, 0, jnp.where(src >= t, src - t, src))
            pltpu.make_async_copy(h2_hbm.at[pl.ds(pl.multiple_of(tok * xt, xt), xt), :],
                                  xbuf.at[pl.ds(pl.multiple_of((s * row_block + r) * xt, xt), xt), :],
                                  gsem.at[s]).start(priority=prio)
        for_rows(row)

    def start_scatter(blk):
        s = blk % Y_BUFFERS

        def row(r, prio):
            src = src_ref[blk * row_block + r]
            dst = jnp.where(src < 0, 2 * t + s * row_block + r, src)
            pltpu.make_async_copy(ybuf.at[pl.ds(pl.multiple_of((s * row_block + r) * rt, rt), rt), :],
                                  y_hbm.at[pl.ds(pl.multiple_of(dst * rt, rt), rt), :],
                                  ssem.at[s]).start(priority=prio)
        for_rows(row)

    def wait_gather(s):
        pltpu.make_async_copy(buf(xbuf, s, xt), buf(xbuf, s, xt), gsem.at[s]).wait()

    def wait_scatter(s):
        pltpu.make_async_copy(buf(ybuf, s, rt), buf(ybuf, s, rt), ssem.at[s]).wait()

    @pl.when(i == 0)
    def _():
        ybuf[...] = jnp.zeros_like(ybuf)
        for s in range(Y_BUFFERS):
            spare = y_hbm.at[pl.ds((2 * t + s * row_block) * rt, row_block * rt), :]
            pltpu.make_async_copy(buf(ybuf, s, rt), spare, ssem.at[s]).start()
        for s in range(Y_BUFFERS):
            wait_scatter(s)
        for first in range(X_BUFFERS - 1):
            @pl.when(first < n_used)
            def _():
                start_gather(first)

    @pl.when((i == 0) | (bexp_ref[i] != bexp_ref[jnp.maximum(i - 1, 0)]))
    def _():
        wg_bf[...] = wg_ref[...].astype(BF16)
        wu_bf[...] = wu_ref[...].astype(BF16)
        wd_bf[...] = wd_ref[...].astype(BF16)

    @pl.when(i + (X_BUFFERS - 1) < n_used)
    def _():
        start_gather(i + (X_BUFFERS - 1))

    @pl.when(i < n_used)
    def _():
        wait_gather(xs)

        @pl.when(i >= Y_BUFFERS)
        def _():
            wait_scatter(ys)

        words = _load_token_major(xbuf, xs * row_block, row_block, xt)
        lo = [lax.bitcast_convert_type(w << 16, F32).astype(BF16) for w in words]
        hi = [lax.bitcast_convert_type(w & HIGH_HALF, F32).astype(BF16) for w in words]
        x = jnp.concatenate(lo + hi, axis=1)
        gt = jnp.dot(x, wg_bf[...], preferred_element_type=F32)
        up = jnp.dot(x, wu_bf[...], preferred_element_type=F32)
        hid = (jax.nn.silu(gt) * up).astype(BF16)
        _store_token_major(ybuf, ys * row_block, jnp.dot(hid, wd_bf[...], preferred_element_type=F32))
        start_scatter(i)

    @pl.when(i == n_blocks - 1)
    def _():
        for back in range(Y_BUFFERS):
            @pl.when(n_used > back)
            def _():
                wait_scatter((n_used - 1 - back) % Y_BUFFERS)


def _experts(h2_tm, t, w_gate, w_up, w_down, layer, bexp, nused, src, row_block):
    d = w_gate.shape[2]
    rt = d // HEAD_DIM
    assert h2_tm.shape == (t * rt // 2, HEAD_DIM) and h2_tm.dtype == jnp.int32
    n_blocks = bexp.shape[0]
    de = w_gate.shape[3]
    kern = functools.partial(_expert_kernel, t=t, rt=rt, row_block=row_block, n_blocks=n_blocks)
    grid_spec = pltpu.PrefetchScalarGridSpec(
        num_scalar_prefetch=3,
        grid=(n_blocks,),
        in_specs=[
            pl.BlockSpec(memory_space=pl.ANY),
            pl.BlockSpec((None, None, d, de), lambda i, be, nu, sr: (layer, be[i], 0, 0)),
            pl.BlockSpec((None, None, d, de), lambda i, be, nu, sr: (layer, be[i], 0, 0)),
            pl.BlockSpec((None, None, de, d), lambda i, be, nu, sr: (layer, be[i], 0, 0)),
        ],
        out_specs=pl.BlockSpec(memory_space=pl.ANY),
        scratch_shapes=[
            pltpu.VMEM((X_BUFFERS * row_block * rt // 2, HEAD_DIM), jnp.int32),
            pltpu.VMEM((Y_BUFFERS * row_block * rt, HEAD_DIM), F32),
            pltpu.VMEM((d, de), BF16),
            pltpu.VMEM((d, de), BF16),
            pltpu.VMEM((de, d), BF16),
            pltpu.SemaphoreType.DMA((X_BUFFERS,)),
            pltpu.SemaphoreType.DMA((Y_BUFFERS,)),
        ],
    )
    return pl.pallas_call(
        kern,
        grid_spec=grid_spec,
        out_shape=jax.ShapeDtypeStruct(((2 * t + Y_BUFFERS * row_block) * rt, HEAD_DIM), F32),
        compiler_params=_params(("arbitrary",)),
        name="moe_experts",
    )(bexp, nused, src, h2_tm, w_gate, w_up, w_down)


def _combine_kernel(x_ref, y0_ref, y1_ref, g_ref, o_ref, *, tm, rt):
    g = g_ref[...]
    g0 = jnp.broadcast_to(g[:, 0:1], (tm, HEAD_DIM))
    g1 = jnp.broadcast_to(g[:, 1:2], (tm, HEAD_DIM))
    for j in range(rt):
        sl = slice(j * HEAD_DIM, (j + 1) * HEAD_DIM)
        y0 = y0_ref[pl.ds(j, tm, stride=rt), :]
        y1 = y1_ref[pl.ds(j, tm, stride=rt), :]
        o_ref[:, sl] = x_ref[:, sl] + (y0 * g0 + y1 * g1)


def _combine(x1, y_tok, gates_t):
    t, d = x1.shape
    rt = d // HEAD_DIM
    tm = _tile(t, 512)
    nb = t // tm
    return pl.pallas_call(
        functools.partial(_combine_kernel, tm=tm, rt=rt),
        grid=(nb,),
        in_specs=[pl.BlockSpec((tm, d), lambda i: (i, 0)),
                  pl.BlockSpec((tm * rt, HEAD_DIM), lambda i: (i, 0)),
                  pl.BlockSpec((tm * rt, HEAD_DIM), lambda i: (i + nb, 0)),
                  pl.BlockSpec((tm, 2), lambda i: (i, 0))],
        out_specs=pl.BlockSpec((tm, d), lambda i: (i, 0)),
        out_shape=jax.ShapeDtypeStruct((t, d), F32),
        compiler_params=_params(("arbitrary",)),
        name="moe_combine",
    )(x1, y_tok, y_tok, gates_t)


MOE_ROW_BLOCK = 256


def kernel(x, norm_mix, w_in, conv_a_w, conv_a_b, lru_wa, lru_ba, lru_wx, lru_bx, lru_lam,
           conv_b_w, q_norm, k_norm, mix_norm, w_out, norm_ffn, w_router_group, b_router_group,
           w_router_expert, b_router_expert, w_gate, w_up, w_down):
    bsz, s, d = x.shape
    t = bsz * s
    depth = w_in.shape[0]
    lru_heads = lru_wa.shape[1]
    d_lru = lru_heads * HEAD_DIM
    d_conv = conv_b_w.shape[-1]
    d_sb =(w_in.shape[-1] - 2 * d_lru - 3 * d_conv) // 3
    sb_heads = d_sb // HEAD_DIM
    n_groups = w_router_group.shape[-1]
    n_experts = w_router_expert.shape[-1]
    assert n_experts // n_groups == EXPERTS_PER_GROUP and n_experts + n_groups <= ROUTE_ROWS
    row_block = MOE_ROW_BLOCK
    n_blocks = -(-(2 * t) // row_block) + n_experts
    assert -(-t // row_block) <= 256

    xt = x.reshape(t, d)
    w_in_bf = w_in.astype(BF16)
    w_out_bf = w_out.astype(BF16)
    for l in range(depth):
        row = lambda v: v[l].reshape(1, -1)
        mix, qkv = _in_projection(xt, row(norm_mix), w_in_bf, l, row(q_norm), row(k_norm), d_sb)
        mix3 = mix.reshape(bsz, s, -1)
        qkv3 = qkv.reshape(bsz, s, -1)
        mg = row(mix_norm)
        w_gates = jnp.concatenate([lru_wa[l], lru_wx[l]], axis=-1).astype(BF16)
        yab = _mixers(mix3, conv_a_w[l], row(conv_a_b), w_gates, row(lru_ba), row(lru_bx),
                      row(lru_lam), conv_b_w[l], mg[:, :d_lru + d_conv])
        yc = _attention(qkv3, mg[:, d_lru + d_conv:], sb_heads)

        pad = jnp.zeros((d, ROUTE_ROWS - n_experts - n_groups), F32)
        w_route_t = jnp.concatenate([w_router_expert[l], w_router_group[l], pad], axis=1).T.astype(BF16)
        b_route = jnp.concatenate([b_router_expert[l], b_router_group[l],
                                   jnp.zeros((ROUTE_ROWS - n_experts - n_groups,), F32)]).reshape(-1, 1)
        x1, h2, logits_t = _out_projection(yab.reshape(t, -1), yc.reshape(t, -1), xt,
                                           w_out_bf, l, row(norm_ffn), w_route_t, b_route)

        dest, gates, meta = _route(logits_t, n_experts, n_groups, row_block)
        src, bexp, nused = _tables(dest.reshape(-1), meta[:, 0], t, n_experts, n_blocks, row_block)
        y_tok = _experts(h2, t, w_gate, w_up, w_down, l, bexp, nused, src, row_block)
        xt = _combine(x1, y_tok, gates.T)
    return xt.reshape(bsz, s, d)
```
